```python
import math, functools
import jax, jax.numpy as jnp
from jax import lax
import numpy as np

D_MODEL = 2048
BATCH = 2
SEQ = 4096
DEPTH = 1
DEC_BATCH = 128
DEC_SEQ = 1
PAST_LEN = 16384
PAGE_SIZE = 128

D_MIX = D_MODEL
MLA_HEADS = 8
MLA_NOPE = 128
MLA_ROPE = 64
MLA_V = 128
Q_LORA = D_MODEL // 4
KV_RANK = D_MODEL // 4
ML_HEADS = 4
ML_DV = (D_MIX - MLA_HEADS * MLA_V) // ML_HEADS
ML_DK = ML_DV // 2
MLSTM_CHUNK = 64
ATTN_QBLOCK = 128
D_FF = ((8 * D_MODEL // 3 + 255) // 256) * 256
CONV_W = 3
ROPE_THETA = 10000.0
EPS = 1e-6
SCALE = (MLA_NOPE + MLA_ROPE) ** -0.5
POOL_NUM = 5
POOL_DEN = 4
F_BIAS = 3.0
PROJ_SIZES = (Q_LORA, KV_RANK, MLA_ROPE, ML_HEADS * ML_DK, ML_HEADS * ML_DK,
              ML_HEADS * ML_DV, ML_HEADS * ML_DV, ML_HEADS, ML_HEADS)
D_PROJ = sum(PROJ_SIZES)

kernel_name = "hymba_mla_mlstm_convffn_step"


def rmsnorm(x, g):
    xf = x.astype(jnp.float32)
    y = xf * lax.rsqrt(jnp.mean(xf * xf, axis=-1, keepdims=True) + EPS) * g.astype(jnp.float32)
    return y.astype(x.dtype)


def rope_tables(pos):
    inv = ROPE_THETA ** (-jnp.arange(0, MLA_ROPE, 2, dtype=jnp.float32) / MLA_ROPE)
    ang = pos.astype(jnp.float32)[:, None] * inv[None, :]
    return jnp.cos(ang), jnp.sin(ang)


def apply_rope(x, cos, sin):
    x1, x2 = jnp.split(x.astype(jnp.float32), 2, axis=-1)
    return jnp.concatenate([x1 * cos - x2 * sin, x1 * sin + x2 * cos], axis=-1).astype(x.dtype)


def split_projection(z):
    idx = [int(v) for v in np.cumsum(PROJ_SIZES)[:-1]]
    return jnp.split(z, idx, axis=-1)


def mlstm_chunkwise(q, k, v, i_pre, f_pre, C0, n0, m0):
    B, S, H, DK = q.shape
    L = math.gcd(S, MLSTM_CHUNK)
    NC = S // L

    def chunks(a):
        a = a.astype(jnp.float32).reshape((B, NC, L, H) + a.shape[3:])
        return jnp.moveaxis(jnp.moveaxis(a, 1, 0), 3, 2)

    tril = jnp.tril(jnp.ones((L, L), dtype=bool))

    def step(carry, xs):
        C, n, m = carry
        qc, kc, vc, ic, fc = xs
        b = jnp.cumsum(jax.nn.log_sigmoid(fc), axis=-1)
        log_d = jnp.where(tril, b[..., :, None] - b[..., None, :] + ic[..., None, :], -jnp.inf)
        log_prev = b + m[..., None]
        m_t = jnp.maximum(log_prev, jnp.max(log_d, axis=-1))
        s = jnp.einsum('bhtd,bhsd->bhts', qc, kc) * jnp.exp(log_d - m_t[..., None])
        w_prev = jnp.exp(log_prev - m_t)
        num = jnp.einsum('bhts,bhsv->bhtv', s, vc) + w_prev[..., None] * jnp.einsum('bhvd,bhtd->bhtv', C, qc)
        den = jnp.sum(s, axis=-1) + w_prev * jnp.einsum('bhd,bhtd->bht', n, qc)
        h = num / jnp.maximum(jnp.abs(den), jnp.exp(-m_t))[..., None]
        m_new = m_t[..., -1]
        w_s = jnp.exp(b[..., -1:] - b + ic - m_new[..., None])
        decay = jnp.exp(b[..., -1] + m - m_new)
        C_new = decay[..., None, None] * C + jnp.einsum('bhs,bhsv,bhsd->bhvd', w_s, vc, kc)
        n_new = decay[..., None] * n + jnp.einsum('bhs,bhsd->bhd', w_s, kc)
        return (C_new, n_new, m_new), h

    init = (C0.astype(jnp.float32), n0.astype(jnp.float32), m0.astype(jnp.float32))
    (C, n, m), h = lax.scan(step, init, (chunks(q), chunks(k), chunks(v), chunks(i_pre), chunks(f_pre)))
    h = jnp.transpose(h, (1, 0, 3, 2, 4)).reshape(B, S, H, v.shape[-1])
    return h, C, n, m


def mla_prompt_attend(q_nope, q_rope, ckv, kr, w_uk, w_uv):
    B, S, H, _ = q_nope.shape
    k_nope = jnp.einsum('bsr,rhd->bshd', ckv, w_uk)
    v = jnp.einsum('bsr,rhd->bshd', ckv, w_uv)
    qb = math.gcd(S, ATTN_QBLOCK)
    nb = S // qb

    def blocks(a):
        return jnp.moveaxis(a.reshape((B, nb, qb) + a.shape[2:]), 1, 0)

    k_pos = jnp.arange(S)

    def body(args):
        qn, qr, q0 = args
        s = (jnp.einsum('bqhd,bkhd->bhqk', qn, k_nope)
             + jnp.einsum('bqhr,bkr->bhqk', qr, kr)).astype(jnp.float32) * SCALE
        q_pos = q0 + jnp.arange(qb)
        s = jnp.where(k_pos[None, :] <= q_pos[:, None], s, -jnp.inf)
        p = jax.nn.softmax(s, axis=-1).astype(v.dtype)
        return jnp.einsum('bhqk,bkhd->bqhd', p, v)

    o = lax.map(body, (blocks(q_nope), blocks(q_rope), jnp.arange(nb) * qb))
    return jnp.moveaxis(o, 0, 1).reshape(B, S, H, MLA_V)


def mla_sample_attend(q_nope, q_rope, ckv, kr, w_uk, w_uv, cache_ckv, cache_kr, page_table, layer):
    B, S, H, _ = q_nope.shape
    q_lat = jnp.einsum('bshd,rhd->bshr', q_nope, w_uk)
    causal = jnp.tril(jnp.ones((S, S), dtype=bool))

    def one_seq(args):
        ql, qr, c_new, k_new, pages = args
        c_past = cache_ckv[layer, pages].reshape(-1, KV_RANK)
        k_past = cache_kr[layer, pages].reshape(-1, MLA_ROPE)
        s_past = (jnp.einsum('qhr,kr->hqk', ql, c_past)
                  + jnp.einsum('qhd,kd->hqk', qr, k_past)).astype(jnp.float32)
        s_new = (jnp.einsum('qhr,kr->hqk', ql, c_new)
                 + jnp.einsum('qhd,kd->hqk', qr, k_new)).astype(jnp.float32)
        s_new = jnp.where(causal, s_new, -jnp.inf)
        s = jnp.concatenate([s_past, s_new], axis=-1) * SCALE
        p = jax.nn.softmax(s, axis=-1).astype(c_new.dtype)
        n_past = c_past.shape[0]
        return (jnp.einsum('hqk,kr->qhr', p[..., :n_past], c_past.astype(c_new.dtype))
                + jnp.einsum('hqk,kr->qhr', p[..., n_past:], c_new))

    o_lat = lax.map(one_seq, (q_lat, q_rope, ckv, kr, page_table))
    return jnp.einsum('bshr,rhd->bshd', o_lat, w_uv)


def decoder_layer(x, pos, attend, C0, n0, m0, conv0,
                  g_mix, w_in, g_q, w_uq, g_kv, w_uk, w_uv, b_i, b_f, g_ml, w_out,
                  g_ffn, w_gate, w_up, w_conv, b_conv, w_down):
    B, S, _ = x.shape
    xn = rmsnorm(x, g_mix)
    c_q, c_kv, k_r, q_m, k_m, v_m, o_m, i_m, f_m = split_projection(xn @ w_in)
    cos, sin = rope_tables(pos)
    q = (rmsnorm(c_q, g_q) @ w_uq).reshape(B, S, MLA_HEADS, MLA_NOPE + MLA_ROPE)
    q_nope = q[..., :MLA_NOPE]
    q_rope = apply_rope(q[..., MLA_NOPE:], cos[:, None, :], sin[:, None, :])
    ckv = rmsnorm(c_kv, g_kv)
    kr = apply_rope(k_r, cos, sin)
    o_mla = attend(q_nope, q_rope, ckv, kr, w_uk, w_uv)
    h, C, n, m = mlstm_chunkwise(
        q_m.reshape(B, S, ML_HEADS, ML_DK),
        (k_m * (ML_DK ** -0.5)).reshape(B, S, ML_HEADS, ML_DK),
        v_m.reshape(B, S, ML_HEADS, ML_DV),
        i_m.astype(jnp.float32) + b_i.astype(jnp.float32),
        f_m.astype(jnp.float32) + b_f.astype(jnp.float32),
        C0, n0, m0)
    h = rmsnorm(h, g_ml.reshape(ML_HEADS, ML_DV)).reshape(B, S, ML_HEADS * ML_DV).astype(x.dtype)
    h = h * jax.nn.sigmoid(o_m)
    x = x + jnp.concatenate([o_mla.reshape(B, S, MLA_HEADS * MLA_V), h], axis=-1) @ w_out
    hn = rmsnorm(x, g_ffn)
    u = hn @ w_gate
    ext = jnp.concatenate([conv0.astype(u.dtype), u], axis=1)
    conv = (ext[:, 0:S] * w_conv[0] + ext[:, 1:S + 1] * w_conv[1]
            + ext[:, 2:S + 2] * w_conv[2] + b_conv)
    y = x + (jax.nn.silu(conv) * (hn @ w_up)) @ w_down
    new_conv = ext[:, S:]
    return y, (ckv, kr, C.astype(x.dtype), n.astype(x.dtype), m.astype(x.dtype), new_conv)


def setup_inputs(seed: int = 0) -> dict:
    key = jax.random.key(seed)
    ks = iter(jax.random.split(key, 40))

    def nrm(shape, scale):
        return jax.random.normal(next(ks), shape, jnp.float32) * scale

    def gain(shape):
        return 1.0 + nrm(shape, 0.02)

    n_pages = PAST_LEN // PAGE_SIZE
    n_pool = (DEC_BATCH * n_pages * POOL_NUM) // POOL_DEN
    page_table = jax.random.permutation(next(ks), n_pool)[: DEC_BATCH * n_pages]
    page_table = page_table.reshape(DEC_BATCH, n_pages).astype(jnp.int32)
    return {
        "x_prompt": nrm((BATCH, SEQ, D_MODEL), 1.0),
        "x_sample": nrm((DEC_BATCH, DEC_SEQ, D_MODEL), 1.0),
        "cache_ckv": nrm((DEPTH, n_pool, PAGE_SIZE, KV_RANK), 1.0),
        "cache_kr": nrm((DEPTH, n_pool, PAGE_SIZE, MLA_ROPE), 1.0),
        "state_C": nrm((DEPTH, DEC_BATCH, ML_HEADS, ML_DV, ML_DK), 0.3),
        "state_n": nrm((DEPTH, DEC_BATCH, ML_HEADS, ML_DK), 0.3),
        "state_m": nrm((DEPTH, DEC_BATCH, ML_HEADS), 1.0),
        "state_conv": nrm((DEPTH, DEC_BATCH, CONV_W - 1, D_FF), 1.0),
        "page_table": page_table,
        "g_mix": gain((DEPTH, D_MODEL)),
        "w_in": nrm((DEPTH, D_MODEL, D_PROJ), D_MODEL ** -0.5),
        "g_q": gain((DEPTH, Q_LORA)),
        "w_uq": nrm((DEPTH, Q_LORA, MLA_HEADS * (MLA_NOPE + MLA_ROPE)), Q_LORA ** -0.5),
        "g_kv": gain((DEPTH, KV_RANK)),
        "w_uk": nrm((DEPTH, KV_RANK, MLA_HEADS, MLA_NOPE), KV_RANK ** -0.5),
        "w_uv": nrm((DEPTH, KV_RANK, MLA_HEADS, MLA_V), KV_RANK ** -0.5),
        "b_i": nrm((DEPTH, ML_HEADS), 0.1),
        "b_f": F_BIAS + nrm((DEPTH, ML_HEADS), 0.1),
        "g_ml": gain((DEPTH, ML_HEADS * ML_DV)),
        "w_out": nrm((DEPTH, D_MIX, D_MODEL), D_MIX ** -0.5),
        "g_ffn": gain((DEPTH, D_MODEL)),
        "w_gate": nrm((DEPTH, D_MODEL, D_FF), D_MODEL ** -0.5),
        "w_up": nrm((DEPTH, D_MODEL, D_FF), D_MODEL ** -0.5),
        "w_conv": nrm((DEPTH, CONV_W, D_FF), CONV_W ** -0.5),
        "b_conv": nrm((DEPTH, D_FF), 0.02),
        "w_down": nrm((DEPTH, D_FF, D_MODEL), D_FF ** -0.5),
        "g_final": gain((D_MODEL,)),
    }


def reference(x_prompt, x_sample, cache_ckv, cache_kr, state_C, state_n, state_m, state_conv,
              page_table, g_mix, w_in, g_q, w_uq, g_kv, w_uk, w_uv, b_i, b_f, g_ml, w_out,
              g_ffn, w_gate, w_up, w_conv, b_conv, w_down, g_final):
    past_len = page_table.shape[1] * cache_ckv.shape[2]
    bp, sp, _ = x_prompt.shape
    bs, ss, _ = x_sample.shape
    pos_p = jnp.arange(sp)
    pos_s = past_len + jnp.arange(ss)
    C0 = jnp.zeros((bp, ML_HEADS, ML_DV, ML_DK), jnp.float32)
    n0 = jnp.zeros((bp, ML_HEADS, ML_DK), jnp.float32)
    m0 = jnp.zeros((bp, ML_HEADS), jnp.float32)
    conv0 = jnp.zeros((bp, CONV_W - 1, D_FF), x_prompt.dtype)
    hp, hs = x_prompt, x_sample
    st_p = [[] for _ in range(6)]
    st_s = [[] for _ in range(6)]
    for l in range(DEPTH):
        lw = (g_mix[l], w_in[l], g_q[l], w_uq[l], g_kv[l], w_uk[l], w_uv[l], b_i[l], b_f[l],
              g_ml[l], w_out[l], g_ffn[l], w_gate[l], w_up[l], w_conv[l], b_conv[l], w_down[l])
        hp, new_p = decoder_layer(hp, pos_p, mla_prompt_attend, C0, n0, m0, conv0, *lw)
        sample_attend = functools.partial(mla_sample_attend, cache_ckv=cache_ckv, cache_kr=cache_kr,
                                          page_table=page_table, layer=l)
        hs, new_s = decoder_layer(hs, pos_s, sample_attend, state_C[l], state_n[l], state_m[l],
                                  state_conv[l], *lw)
        for j in range(6):
            st_p[j].append(new_p[j])
            st_s[j].append(new_s[j])
    y_prompt = rmsnorm(hp, g_final)
    y_sample = rmsnorm(hs, g_final)
    ckv_p, kr_p, C_p, n_p, m_p, conv_p = [jnp.stack(a, axis=0) for a in st_p]
    ckv_s, kr_s, C_s, n_s, m_s, conv_s = [jnp.stack(a, axis=0) for a in st_s]
    return (y_prompt, y_sample, ckv_p, kr_p, C_p, n_p, m_p, conv_p,
            ckv_s, kr_s, C_s, n_s, m_s, conv_s)
```

```python
import functools
import math

import jax
import jax.numpy as jnp
from jax import lax
from jax.experimental import pallas as pl
from jax.experimental.pallas import tpu as pltpu

F32 = jnp.float32
BF16 = jnp.bfloat16

MLA_HEADS = 8
MLA_NOPE = 128
MLA_ROPE = 64
MLA_V = 128
ML_HEADS = 4
CONV_W = 3
ROPE_THETA = 10000.0
EPS = 1e-6
SCALE = (MLA_NOPE + MLA_ROPE) ** -0.5

LANES = 128
QK_SLOT = 2 * LANES
VMEM_LIMIT = 56 * 1024 * 1024


def _cparams(sem, vmem=VMEM_LIMIT):
    return pltpu.CompilerParams(dimension_semantics=sem, vmem_limit_bytes=vmem)


def _rms(x, g):
    return x * lax.rsqrt(jnp.mean(x * x, axis=-1, keepdims=True) + EPS) * g


def _sigmoid(x):
    return 1.0 / (1.0 + jnp.exp(-x))


def _log_sigmoid(x):
    return jnp.minimum(x, 0.0) - jnp.log1p(jnp.exp(-jnp.abs(x)))


def _dot(a, b):
    return jnp.dot(a, b, preferred_element_type=F32)


def _dot_nt(a, b):
    return lax.dot_general(a, b, (((1,), (1,)), ((), ())), preferred_element_type=F32)


def _dot_tn(a, b):
    return lax.dot_general(a, b, (((0,), (0,)), ((), ())), preferred_element_type=F32)


def _proj_in_kernel(x_ref, g_ref, w_ref, z_ref, xn_ref):
    @pl.when(pl.program_id(1) == 0)
    def _():
        xn_ref[...] = _rms(x_ref[...], g_ref[...]).astype(BF16)

    z_ref[...] = _dot(xn_ref[...], w_ref[...])


def _proj_in(x, g, w, tm, tn):
    t, d = x.shape
    n = w.shape[1]
    return pl.pallas_call(
        _proj_in_kernel,
        out_shape=jax.ShapeDtypeStruct((t, n), F32),
        grid=(t // tm, n // tn),
        in_specs=[pl.BlockSpec((tm, d), lambda i, j: (i, 0)),
                  pl.BlockSpec((1, d), lambda i, j: (0, 0)),
                  pl.BlockSpec((d, tn), lambda i, j: (0, j))],
        out_specs=pl.BlockSpec((tm, tn), lambda i, j: (i, j)),
        scratch_shapes=[pltpu.VMEM((tm, d), BF16)],
        compiler_params=_cparams(("arbitrary", "arbitrary")),
        name="proj_in",
    )(x, g, w)


def _rope128(x, cos, sin):
    return x * cos + pltpu.roll(x, 64, axis=1) * sin


def _mla_proj_kernel(cq_ref, ckv_ref, tail_ref, gq_ref, gkv_ref, wq_ref, wk_ref, wv_ref,
                     cos_ref, sin_ref, q_ref, k_ref, v_ref, ckv_out_ref, kr_out_ref):
    cos = cos_ref[...]
    sin = sin_ref[...]
    cqn = _rms(cq_ref[...], gq_ref[...]).astype(BF16)
    q = _dot(cqn, wq_ref[...])
    ckv = _rms(ckv_ref[...], gkv_ref[...])
    ckv_out_ref[...] = ckv
    ckv_b = ckv.astype(BF16)
    kn = _dot(ckv_b, wk_ref[...])
    v_ref[...] = _dot(ckv_b, wv_ref[...]).astype(BF16)
    kr = _rope128(tail_ref[...], cos, sin)
    kr_out_ref[...] = kr[:, :MLA_ROPE]
    kr_b = kr.astype(BF16)
    for h in range(MLA_HEADS):
        q0 = h * QK_SLOT
        q_ref[:, q0:q0 + LANES] = q[:, q0:q0 + LANES].astype(BF16)
        q_ref[:, q0 + LANES:q0 + QK_SLOT] = _rope128(
            q[:, q0 + LANES:q0 + QK_SLOT], cos, sin).astype(BF16)
        k_ref[:, q0:q0 + LANES] = kn[:, h * MLA_NOPE:(h + 1) * MLA_NOPE].astype(BF16)
        k_ref[:, q0 + LANES:q0 + QK_SLOT] = kr_b


def _mla_proj(z, g_q, g_kv, wq, wk, wv, cos, sin, tm):
    t = z.shape[0]
    rows = cos.shape[0] // tm
    hq = MLA_HEADS * QK_SLOT
    kvr = wk.shape[0]
    const = lambda i: (0, 0)
    return pl.pallas_call(
        _mla_proj_kernel,
        out_shape=(jax.ShapeDtypeStruct((t, hq), BF16),
                   jax.ShapeDtypeStruct((t, hq), BF16),
                   jax.ShapeDtypeStruct((t, MLA_HEADS * MLA_V), BF16),
                   jax.ShapeDtypeStruct((t, kvr), F32),
                   jax.ShapeDtypeStruct((t, MLA_ROPE), F32)),
        grid=(t // tm,),
        in_specs=[pl.BlockSpec((tm, 512), lambda i: (i, 0)),
                  pl.BlockSpec((tm, 512), lambda i: (i, 1)),
                  pl.BlockSpec((tm, LANES), lambda i: (i, 32)),
                  pl.BlockSpec((1, 512), const),
                  pl.BlockSpec((1, 512), const),
                  pl.BlockSpec(wq.shape, const),
                  pl.BlockSpec(wk.shape, const),
                  pl.BlockSpec(wv.shape, const),
                  pl.BlockSpec((tm, LANES), lambda i: (i % rows, 0)),
                  pl.BlockSpec((tm, LANES), lambda i: (i % rows, 0))],
        out_specs=(pl.BlockSpec((tm, hq), lambda i: (i, 0)),
                   pl.BlockSpec((tm, hq), lambda i: (i, 0)),
                   pl.BlockSpec((tm, MLA_HEADS * MLA_V), lambda i: (i, 0)),
                   pl.BlockSpec((tm, kvr), lambda i: (i, 0)),
                   pl.BlockSpec((tm, MLA_ROPE), lambda i: (i, 0))),
        compiler_params=_cparams(("arbitrary",)),
        name="mla_proj",
    )(z, z, z, g_q, g_kv, wq, wk, wv, cos, sin)


def _flash_kernel(q_ref, k_ref, v_ref, o_ref, m_ref, l_ref, acc_ref):
    qi = pl.program_id(2)
    kj = pl.program_id(3)

    @pl.when(kj == 0)
    def _():
        m_ref[...] = jnp.full(m_ref.shape, -jnp.inf, F32)
        l_ref[...] = jnp.zeros(l_ref.shape, F32)
        acc_ref[...] = jnp.zeros(acc_ref.shape, F32)

    def update(s):
        m_prev = m_ref[...]
        m_new = jnp.maximum(m_prev, jnp.max(s, axis=1, keepdims=True))
        alpha = jnp.exp(m_prev - m_new)
        p = jnp.exp(s - m_new)
        l_ref[...] = alpha * l_ref[...] + jnp.sum(p, axis=1, keepdims=True)
        acc_ref[...] = alpha * acc_ref[...] + _dot(p.astype(BF16), v_ref[...])
        m_ref[...] = m_new

    @pl.when(kj < qi)
    def _():
        update(_dot_nt(q_ref[...], k_ref[...]) * SCALE)

    @pl.when(kj == qi)
    def _():
        s = _dot_nt(q_ref[...], k_ref[...]) * SCALE
        row = lax.broadcasted_iota(jnp.int32, s.shape, 0)
        col = lax.broadcasted_iota(jnp.int32, s.shape, 1)
        update(jnp.where(col <= row, s, -jnp.inf))
        o_ref[...] = (acc_ref[...] / l_ref[...]).astype(o_ref.dtype)


def _flash(q, k, v, batch, seq, tq):
    nb = seq // tq
    t = q.shape[0]
    return pl.pallas_call(
        _flash_kernel,
        out_shape=jax.ShapeDtypeStruct((t, MLA_HEADS * MLA_V), BF16),
        grid=(batch, MLA_HEADS, nb, nb),
        in_specs=[pl.BlockSpec((tq, QK_SLOT), lambda b, h, i, j: (b * nb + i, h)),
                  pl.BlockSpec((tq, QK_SLOT), lambda b, h, i, j: (b * nb + jnp.minimum(i, j), h)),
                  pl.BlockSpec((tq, MLA_V), lambda b, h, i, j: (b * nb + jnp.minimum(i, j), h))],
        out_specs=pl.BlockSpec((tq, MLA_V), lambda b, h, i, j: (b * nb + i, h)),
        scratch_shapes=[pltpu.VMEM((tq, 1), F32), pltpu.VMEM((tq, 1), F32),
                        pltpu.VMEM((tq, MLA_V), F32)],
        compiler_params=_cparams(("arbitrary",) * 4),
        name="flash_attn",
    )(q, k, v)


def _mlstm_chunk_kernel(q_ref, k_ref, v_ref, o_ref, gc_ref, gr_ref, bias_r_ref, bias_c_ref,
                        gml_ref, h_ref, c_out_ref, n_out_ref, m_out_ref,
                        c_sc, n_sc, m_sc, *, dk, dv):
    c_idx = pl.program_id(1)
    L = q_ref.shape[0]
    nh = ML_HEADS

    @pl.when(c_idx == 0)
    def _():
        c_sc[...] = jnp.zeros(c_sc.shape, F32)
        n_sc[...] = jnp.zeros(n_sc.shape, F32)
        m_sc[...] = jnp.zeros(m_sc.shape, F32)

    gates_c = gc_ref[...][:, :2 * nh] + bias_r_ref[...]
    gates_r = gr_ref[...] + bias_c_ref[...]
    row = lax.broadcasted_iota(jnp.int32, (L, L), 0)
    col = lax.broadcasted_iota(jnp.int32, (L, L), 1)
    lower = col <= row
    lower_f = lower.astype(F32)
    upper_f = (row <= col).astype(F32)
    hp = lax.Precision.HIGHEST
    b_col = jnp.dot(lower_f, _log_sigmoid(gates_c[:, nh:]), precision=hp,
                    preferred_element_type=F32)
    b_row = jnp.dot(_log_sigmoid(gates_r[nh:, :]), upper_f, precision=hp,
                    preferred_element_type=F32)

    for h in range(nh):
        bc = b_col[:, h:h + 1]
        br = b_row[h:h + 1, :]
        ic = gates_c[:, h:h + 1]
        ir = gates_r[h:h + 1, :]
        m_prev = m_sc[h][0:1, 0:1]
        log_d = jnp.where(lower, bc - br + ir, -jnp.inf)
        log_prev = bc + m_prev
        m_t = jnp.maximum(log_prev, jnp.max(log_d, axis=1, keepdims=True))
        q = q_ref[:, h * dk:(h + 1) * dk]
        k = k_ref[:, h * dk:(h + 1) * dk] * (dk ** -0.5)
        v = v_ref[:, h * dv:(h + 1) * dv]
        qb = q.astype(BF16)
        kb = k.astype(BF16)
        s = _dot_nt(qb, kb) * jnp.exp(log_d - m_t)
        w_prev = jnp.exp(log_prev - m_t)
        c_prev = c_sc[h]
        n_prev = n_sc[h][0:1, :]
        num = _dot(s.astype(BF16), v.astype(BF16)) + w_prev * _dot_nt(qb, c_prev.astype(BF16))
        den = jnp.sum(s, axis=1, keepdims=True) + w_prev * jnp.sum(q * n_prev, axis=1, keepdims=True)
        hh = num / jnp.maximum(jnp.abs(den), jnp.exp(-m_t))
        hh = _rms(hh, gml_ref[:, h * dv:(h + 1) * dv])
        h_ref[:, h * dv:(h + 1) * dv] = (hh * _sigmoid(o_ref[:, h * dv:(h + 1) * dv])).astype(h_ref.dtype)
        m_new = m_t[L - 1:L, :]
        b_last = bc[L - 1:L, :]
        w_s = jnp.exp(b_last - bc + ic - m_new)
        decay = jnp.exp(b_last + m_prev - m_new)
        c_sc[h] = decay * c_prev + _dot_tn((w_s * v).astype(BF16), kb)
        n_sc[h] = jnp.broadcast_to(decay * n_prev + jnp.sum(w_s * k, axis=0, keepdims=True),
                                   n_sc.shape[1:])
        m_sc[h] = jnp.broadcast_to(m_new, m_sc.shape[1:])

    @pl.when(c_idx == pl.num_programs(1) - 1)
    def _():
        c_out_ref[0] = c_sc[...]
        n_out_ref[0] = n_sc[...]
        m_out_ref[0] = m_sc[...]


def _mlstm_prompt(z, gates_row, bias_r, bias_c, g_ml, batch, seq, chunk, dk, dv):
    t = z.shape[0]
    nc = seq // chunk
    nh = ML_HEADS
    kern = functools.partial(_mlstm_chunk_kernel, dk=dk, dv=dv)
    tok = lambda w, blk: pl.BlockSpec((chunk, w), lambda b, c: (b * nc + c, blk))
    const = lambda b, c: (0, 0)
    st = lambda b, c: (b, 0, 0, 0)
    return pl.pallas_call(
        kern,
        out_shape=(jax.ShapeDtypeStruct((t, nh * dv), BF16),
                   jax.ShapeDtypeStruct((batch, nh, dv, dk), F32),
                   jax.ShapeDtypeStruct((batch, nh, 8, dk), F32),
                   jax.ShapeDtypeStruct((batch, nh, 8, LANES), F32)),
        grid=(batch, nc),
        in_specs=[tok(nh * dk, 2), tok(nh * dk, 3), tok(nh * dv, 2), tok(nh * dv, 3),
                  tok(LANES, 33),
                  pl.BlockSpec((2 * nh, chunk), lambda b, c: (0, b * nc + c)),
                  pl.BlockSpec((1, 2 * nh), const),
                  pl.BlockSpec((2 * nh, 1), const),
                  pl.BlockSpec((1, nh * dv), const)],
        out_specs=(pl.BlockSpec((chunk, nh * dv), lambda b, c: (b * nc + c, 0)),
                   pl.BlockSpec((1, nh, dv, dk), st),
                   pl.BlockSpec((1, nh, 8, dk), st),
                   pl.BlockSpec((1, nh, 8, LANES), st)),
        scratch_shapes=[pltpu.VMEM((nh, dv, dk), F32), pltpu.VMEM((nh, 8, dk), F32),
                        pltpu.VMEM((nh, 8, LANES), F32)],
        compiler_params=_cparams(("arbitrary", "arbitrary")),
        name="mlstm_chunk",
    )(z, z, z, z, z, gates_row, bias_r, bias_c, g_ml)


def _mlstm_step_kernel(q_ref, k_ref, v_ref, o_ref, g_ref, bias_ref, gml_ref, c0_ref, n0_ref,
                       m0_ref, h_ref, c_ref, n_ref, m_ref, *, dk, dv):
    nh = ML_HEADS
    gates = g_ref[0][:, :2 * nh] + bias_ref[...]
    m_all = m0_ref[0]
    n_all = n0_ref[0]
    m_cols = []
    for h in range(nh):
        i_g = gates[:, h:h + 1]
        b = _log_sigmoid(gates[:, nh + h:nh + h + 1])
        m_prev = m_all[:, h:h + 1]
        q = q_ref[0][:, h * dk:(h + 1) * dk]
        k = k_ref[0][:, h * dk:(h + 1) * dk] * (dk ** -0.5)
        v = v_ref[0][:, h * dv:(h + 1) * dv]
        qb = q.astype(BF16)
        kb = k.astype(BF16)
        log_prev = b + m_prev
        m_t = jnp.maximum(log_prev, i_g)
        qk = jnp.sum(qb.astype(F32) * kb.astype(F32), axis=1, keepdims=True)
        s = qk * jnp.exp(i_g - m_t)
        w_prev = jnp.exp(log_prev - m_t)
        c_prev = c0_ref[0, h]
        n_prev = n_all[h:h + 1, :]
        q8 = jnp.broadcast_to(qb, (8, dk))
        cq = _dot_nt(q8, c_prev.astype(BF16))[0:1, :]
        num = s * v.astype(BF16).astype(F32) + w_prev * cq
        den = s + w_prev * jnp.sum(q * n_prev, axis=1, keepdims=True)
        hh = num / jnp.maximum(jnp.abs(den), jnp.exp(-m_t))
        hh = _rms(hh, gml_ref[:, h * dv:(h + 1) * dv])
        h_ref[0, :, h * dv:(h + 1) * dv] = (hh * _sigmoid(o_ref[0][:, h * dv:(h + 1) * dv])).astype(h_ref.dtype)
        w_s = jnp.exp(i_g - m_t)
        decay = jnp.exp(log_prev - m_t)
        rows = lax.broadcasted_iota(jnp.int32, (8, dv), 0)
        wv8 = jnp.where(rows == 0, jnp.broadcast_to(w_s * v, (8, dv)), 0.0).astype(BF16)
        k8 = jnp.broadcast_to(kb, (8, dk))
        c_ref[0, h] = decay * c_prev + _dot_tn(wv8, k8)
        n_ref[0, h:h + 1, :] = decay * n_prev + w_s * k
        m_cols.append(m_t)
    m_ref[0] = jnp.concatenate(m_cols, axis=1)


def _mlstm_sample(z3, bias_r, g_ml, c0, n0, m0, dk, dv):
    nb = z3.shape[0]
    nh = ML_HEADS
    kern = functools.partial(_mlstm_step_kernel, dk=dk, dv=dv)
    tok = lambda w, blk: pl.BlockSpec((1, 1, w), lambda b: (b, 0, blk))
    const = lambda b: (0, 0)
    return pl.pallas_call(
        kern,
        out_shape=(jax.ShapeDtypeStruct((nb, 1, nh * dv), BF16),
                   jax.ShapeDtypeStruct((nb, nh, dv, dk), F32),
                   jax.ShapeDtypeStruct((nb, nh, dk), F32),
                   jax.ShapeDtypeStruct((nb, 1, nh), F32)),
        grid=(nb,),
        in_specs=[tok(nh * dk, 2), tok(nh * dk, 3), tok(nh * dv, 2), tok(nh * dv, 3),
                  tok(LANES, 33),
                  pl.BlockSpec((1, 2 * nh), const),
                  pl.BlockSpec((1, nh * dv), const),
                  pl.BlockSpec((1, nh, dv, dk), lambda b: (b, 0, 0, 0)),
                  pl.BlockSpec((1, nh, dk), lambda b: (b, 0, 0)),
                  pl.BlockSpec((1, 1, nh), lambda b: (b, 0, 0))],
        out_specs=(pl.BlockSpec((1, 1, nh * dv), lambda b: (b, 0, 0)),
                   pl.BlockSpec((1, nh, dv, dk), lambda b: (b, 0, 0, 0)),
                   pl.BlockSpec((1, nh, dk), lambda b: (b, 0, 0)),
                   pl.BlockSpec((1, 1, nh), lambda b: (b, 0, 0))),
        compiler_params=_cparams(("arbitrary",)),
        name="mlstm_step",
    )(z3, z3, z3, z3, z3, bias_r, g_ml, c0, n0, m0)


def _q_absorb_kernel(q_ref, wk_ref, o_ref):
    o_ref[...] = _dot_nt(q_ref[:, :MLA_NOPE], wk_ref[...].astype(BF16))


def _q_absorb(q_cat, w_uk_flat):
    t = q_cat.shape[0]
    r = w_uk_flat.shape[0]
    return pl.pallas_call(
        _q_absorb_kernel,
        out_shape=jax.ShapeDtypeStruct((t, MLA_HEADS * r), F32),
        grid=(MLA_HEADS,),
        in_specs=[pl.BlockSpec((t, QK_SLOT), lambda h: (0, h)),
                  pl.BlockSpec((r, MLA_NOPE), lambda h: (0, h))],
        out_specs=pl.BlockSpec((t, r), lambda h: (0, h)),
        compiler_params=_cparams(("arbitrary",)),
        name="q_absorb",
    )(q_cat, w_uk_flat)


def _o_absorb_kernel(o_ref, wv_ref, out_ref):
    out_ref[...] = _dot(o_ref[...].astype(BF16), wv_ref[...].astype(BF16)).astype(out_ref.dtype)


def _o_absorb(o_lat2d, w_uv_flat):
    t = o_lat2d.shape[0]
    r = w_uv_flat.shape[0]
    return pl.pallas_call(
        _o_absorb_kernel,
        out_shape=jax.ShapeDtypeStruct((t, MLA_HEADS * MLA_V), BF16),
        grid=(MLA_HEADS,),
        in_specs=[pl.BlockSpec((t, r), lambda h: (0, h)),
                  pl.BlockSpec((r, MLA_V), lambda h: (0, h))],
        out_specs=pl.BlockSpec((t, MLA_V), lambda h: (0, h)),
        compiler_params=_cparams(("arbitrary",)),
        name="o_absorb",
    )(o_lat2d, w_uv_flat)


def _mla_decode_kernel(pt_ref, ql_ref, qr_ref, cn_ref, kn_ref, cc_hbm, ck_hbm, o_ref,
                       cbuf, kbuf, sem, *, layer, n_seq, n_chunks, pages_per_chunk, page_size):
    total = n_seq * n_chunks
    pages_per_seq = n_chunks * pages_per_chunk

    def chunk_copies(step, slot):
        b = step // n_chunks
        c = step - b * n_chunks
        base = b * pages_per_seq + c * pages_per_chunk
        cps = []
        for j in range(pages_per_chunk):
            page = pt_ref[base + j]
            dst = pl.ds(j * page_size, page_size)
            cps.append(pltpu.make_async_copy(cc_hbm.at[layer, page], cbuf.at[slot, dst], sem.at[0, slot]))
            cps.append(pltpu.make_async_copy(ck_hbm.at[layer, page], kbuf.at[slot, dst], sem.at[1, slot]))
        return cps

    for cp in chunk_copies(0, 0):
        cp.start()

    def seq_body(b, _):
        ql = ql_ref[b]
        qr = qr_ref[b]
        ql_b = ql.astype(BF16)
        qr_b = qr.astype(BF16)

        def chunk_body(c, carry):
            m, l, acc = carry
            step = b * n_chunks + c
            slot = lax.rem(step, 2)

            @pl.when(step + 1 < total)
            def _():
                for cp in chunk_copies(step + 1, 1 - slot):
                    cp.start()

            for cp in chunk_copies(step, slot):
                cp.wait()
            cb = cbuf[slot].astype(BF16)
            kb = kbuf[slot].astype(BF16)
            s = (_dot_nt(ql_b, cb) + _dot_nt(qr_b, kb)) * SCALE
            m_new = jnp.maximum(m, jnp.max(s, axis=1, keepdims=True))
            alpha = jnp.exp(m - m_new)
            p = jnp.exp(s - m_new)
            l = alpha * l + jnp.sum(p, axis=1, keepdims=True)
            acc = alpha * acc + _dot(p.astype(BF16), cb)
            return m_new, l, acc

        nh = ql.shape[0]
        init = (jnp.full((nh, 1), -jnp.inf, F32), jnp.zeros((nh, 1), F32),
                jnp.zeros(ql.shape, F32))
        m, l, acc = lax.fori_loop(0, n_chunks, chunk_body, init)
        cn = cn_ref[b]
        kn = kn_ref[b]
        s_new = (jnp.sum(ql * cn, axis=1, keepdims=True)
                 + jnp.sum(qr * kn, axis=1, keepdims=True)) * SCALE
        m_f = jnp.maximum(m, s_new)
        alpha = jnp.exp(m - m_f)
        p_new = jnp.exp(s_new - m_f)
        o_ref[b] = (alpha * acc + p_new * cn) / (alpha * l + p_new)
        return 0

    lax.fori_loop(0, n_seq, seq_body, 0)


def _mla_decode(page_table, q_lat, q_rope, ckv_new, kr_new, cache_ckv, cache_kr, layer,
                pages_per_chunk):
    n_seq, n_pages = page_table.shape
    page_size, r = cache_ckv.shape[2], cache_ckv.shape[3]
    rope = cache_kr.shape[3]
    n_chunks = n_pages // pages_per_chunk
    kc = pages_per_chunk * page_size
    kern = functools.partial(_mla_decode_kernel, layer=layer, n_seq=n_seq, n_chunks=n_chunks,
                             pages_per_chunk=pages_per_chunk, page_size=page_size)
    vmem = pl.BlockSpec(memory_space=pltpu.VMEM)
    return pl.pallas_call(
        kern,
        out_shape=jax.ShapeDtypeStruct(q_lat.shape, F32),
        in_specs=[pl.BlockSpec(memory_space=pltpu.SMEM), vmem, vmem, vmem, vmem,
                  pl.BlockSpec(memory_space=pl.ANY), pl.BlockSpec(memory_space=pl.ANY)],
        out_specs=vmem,
        scratch_shapes=[pltpu.VMEM((2, kc, r), F32), pltpu.VMEM((2, kc, rope), F32),
                        pltpu.SemaphoreType.DMA((2, 2))],
        compiler_params=pltpu.CompilerParams(vmem_limit_bytes=VMEM_LIMIT),
        name="mla_decode",
    )(page_table.reshape(-1), q_lat, q_rope, ckv_new, kr_new, cache_ckv, cache_kr)


def _out_proj_kernel(a_ref, h_ref, w1_ref, w2_ref, x_ref, o_ref):
    o_ref[...] = (x_ref[...] + _dot(a_ref[...], w1_ref[...].astype(BF16))
                  + _dot(h_ref[...], w2_ref[...].astype(BF16)))


def _out_proj(a, h, w_out, x, tm, tn):
    t, d = x.shape
    ka = a.shape[1]
    kh = h.shape[1]
    assert ka == kh
    return pl.pallas_call(
        _out_proj_kernel,
        out_shape=jax.ShapeDtypeStruct((t, d), F32),
        grid=(t // tm, d // tn),
        in_specs=[pl.BlockSpec((tm, ka), lambda i, j: (i, 0)),
                  pl.BlockSpec((tm, kh), lambda i, j: (i, 0)),
                  pl.BlockSpec((ka, tn), lambda i, j: (0, j)),
                  pl.BlockSpec((kh, tn), lambda i, j: (1, j)),
                  pl.BlockSpec((tm, tn), lambda i, j: (i, j))],
        out_specs=pl.BlockSpec((tm, tn), lambda i, j: (i, j)),
        compiler_params=_cparams(("arbitrary", "arbitrary")),
        name="out_proj",
    )(a, h, w_out, w_out, x)


def _ffn_up_kernel(*refs, seq_tiles, decode):
    if decode:
        (x_ref, g_ref, wg_ref, wu_ref, wc_ref, bc_ref, h0_ref, h1_ref,
         a_ref, u_ref, hn_ref) = refs
    else:
        (x_ref, g_ref, wg_ref, wu_ref, wc_ref, bc_ref,
         a_ref, u_ref, hn_ref, prev_ref) = refs
    i = pl.program_id(0)
    f = pl.program_id(1)

    @pl.when(f == 0)
    def _():
        hn_ref[...] = _rms(x_ref[...], g_ref[...]).astype(BF16)

    hn = hn_ref[...]
    u = _dot(hn, wg_ref[...].astype(BF16))
    up = _dot(hn, wu_ref[...].astype(BF16))
    tm = u.shape[0]
    tail = u_ref.shape[0]
    u_ref[...] = u[tm - tail:, :]
    wc = wc_ref[...]
    if decode:
        u_m2 = h0_ref[...]
        u_m1 = h1_ref[...]
    else:
        @pl.when(i == 0)
        def _():
            prev_ref[f] = jnp.zeros(prev_ref.shape[1:], F32)

        prev = jnp.where(lax.rem(i, seq_tiles) != 0, prev_ref[f], 0.0)
        row = lax.broadcasted_iota(jnp.int32, u.shape, 0)
        p6 = jnp.broadcast_to(prev[6:7, :], u.shape)
        p7 = jnp.broadcast_to(prev[7:8, :], u.shape)
        u_m1 = jnp.where(row == 0, p7, pltpu.roll(u, 1, axis=0))
        u_m2 = jnp.where(row == 0, p6, jnp.where(row == 1, p7, pltpu.roll(u, 2, axis=0)))
        prev_ref[f] = u[tm - 8:, :]
    conv = u_m2 * wc[0:1, :] + u_m1 * wc[1:2, :] + u * wc[2:3, :] + bc_ref[...]
    a_ref[...] = (conv * _sigmoid(conv) * up).astype(a_ref.dtype)


def _ffn_up(x, g, w_gate, w_up, w_conv, b_conv, tm, tf, seq=None, hist=None):
    t, d = x.shape
    ff = w_gate.shape[1]
    decode = hist is not None
    tail = tm if decode else 8
    nf = ff // tf
    kern = functools.partial(_ffn_up_kernel, seq_tiles=None if decode else seq // tm,
                             decode=decode)
    in_specs = [pl.BlockSpec((tm, d), lambda i, f: (i, 0)),
                pl.BlockSpec((1, d), lambda i, f: (0, 0)),
                pl.BlockSpec((d, tf), lambda i, f: (0, f)),
                pl.BlockSpec((d, tf), lambda i, f: (0, f)),
                pl.BlockSpec((CONV_W, tf), lambda i, f: (0, f)),
                pl.BlockSpec((1, tf), lambda i, f: (0, f))]
    args = [x, g, w_gate, w_up, w_conv, b_conv]
    scratch = [pltpu.VMEM((tm, d), BF16)]
    if decode:
        in_specs += [pl.BlockSpec((tm, tf), lambda i, f: (i, f))] * 2
        args += list(hist)
    else:
        scratch.append(pltpu.VMEM((nf, 8, tf), F32))
    return pl.pallas_call(
        kern,
        out_shape=(jax.ShapeDtypeStruct((t, ff), BF16),
                   jax.ShapeDtypeStruct((t // tm * tail, ff), F32)),
        grid=(t // tm, nf),
        in_specs=in_specs,
        out_specs=(pl.BlockSpec((tm, tf), lambda i, f: (i, f)),
                   pl.BlockSpec((tail, tf), lambda i, f: (i, f))),
        scratch_shapes=scratch,
        compiler_params=_cparams(("arbitrary", "arbitrary")),
        name="ffn_up",
    )(*args)


def _ffn_down_kernel(a_ref, w_ref, x_ref, g_ref, y_ref, acc_ref):
    k = pl.program_id(1)

    @pl.when(k == 0)
    def _():
        acc_ref[...] = x_ref[...]

    acc_ref[...] += _dot(a_ref[...], w_ref[...].astype(BF16))

    @pl.when(k == pl.num_programs(1) - 1)
    def _():
        y_ref[...] = _rms(acc_ref[...], g_ref[...])


def _ffn_down(a, w_down, x, g_final, tm, tk):
    t, d = x.shape
    ff = a.shape[1]
    return pl.pallas_call(
        _ffn_down_kernel,
        out_shape=jax.ShapeDtypeStruct((t, d), F32),
        grid=(t // tm, ff // tk),
        in_specs=[pl.BlockSpec((tm, tk), lambda i, k: (i, k)),
                  pl.BlockSpec((tk, d), lambda i, k: (k, 0)),
                  pl.BlockSpec((tm, d), lambda i, k: (i, 0)),
                  pl.BlockSpec((1, d), lambda i, k: (0, 0))],
        out_specs=pl.BlockSpec((tm, d), lambda i, k: (i, 0)),
        scratch_shapes=[pltpu.VMEM((tm, d), F32)],
        compiler_params=_cparams(("arbitrary", "arbitrary")),
        name="ffn_down",
    )(a, w_down, x, g_final)


def _rope_tables(pos):
    inv = ROPE_THETA ** (-jnp.arange(0, MLA_ROPE, 2, dtype=F32) / MLA_ROPE)
    ang = pos.astype(F32)[:, None] * inv[None, :]
    cos, sin = jnp.cos(ang), jnp.sin(ang)
    zero = jnp.zeros_like(cos)
    return (jnp.concatenate([cos, cos, zero, zero], axis=1),
            jnp.concatenate([-sin, sin, zero, zero], axis=1))


def _prep_w_in(w_in, q_lora, kv_rank, dk, dv):
    nh = ML_HEADS
    o_kr = q_lora + kv_rank
    o_qm = o_kr + MLA_ROPE
    o_gate = o_qm + 2 * nh * dk + 2 * nh * dv
    half = MLA_ROPE // 2
    pad = jnp.zeros((w_in.shape[0], LANES - 2 * nh), w_in.dtype)
    cols = [w_in[:, :o_kr], w_in[:, o_qm:o_gate],
            w_in[:, o_kr:o_qm], w_in[:, o_kr + half:o_qm], w_in[:, o_kr:o_kr + half],
            w_in[:, o_gate:], pad]
    return jnp.concatenate(cols, axis=1).astype(BF16)


def _prep_w_uq(w_uq):
    hd = MLA_NOPE + MLA_ROPE
    half = MLA_ROPE // 2
    cols = []
    for h in range(MLA_HEADS):
        o = h * hd
        cols += [w_uq[:, o:o + hd], w_uq[:, o + MLA_NOPE + half:o + hd],
                 w_uq[:, o + MLA_NOPE:o + MLA_NOPE + half]]
    return jnp.concatenate(cols, axis=1).astype(BF16)


def _layer(x, pos, lw, prep, mode, state):
    (g_mix, w_in, g_q, w_uq, g_kv, w_uk, w_uv, b_i, b_f, g_ml, w_out,
     g_ffn, w_gate, w_up, w_conv, b_conv, w_down) = lw
    w_in_p, w_uq_p, w_uk_f, w_uv_f = prep
    t, d = x.shape
    nh = ML_HEADS
    dk = (w_in.shape[1] - g_q.shape[0] - g_kv.shape[0] - MLA_ROPE - 2 * nh) // (6 * nh)
    dv = 2 * dk
    prompt = mode == "prompt"
    tm = 512 if prompt else t
    cos, sin = _rope_tables(pos)
    if not prompt:
        cos = jnp.broadcast_to(cos, (t, LANES))
        sin = jnp.broadcast_to(sin, (t, LANES))

    z = _proj_in(x, g_mix[None, :], w_in_p, tm, w_in_p.shape[1] // 2)
    q_cat, k_cat, v, ckv, kr = _mla_proj(
        z, g_q[None, :], g_kv[None, :], w_uq_p, w_uk_f.astype(BF16), w_uv_f.astype(BF16),
        cos, sin, tm)
    bias_r = jnp.concatenate([b_i, b_f])[None, :].astype(F32)

    if prompt:
        batch, seq = state["batch"], state["seq"]
        o_mla = _flash(q_cat, k_cat, v, batch, seq, min(seq, 1024))
        gates_row = z[:, 33 * LANES:33 * LANES + 2 * nh].T
        h_ml, c_new, n_new, m_new = _mlstm_prompt(
            z, gates_row, bias_r, bias_r.T, g_ml[None, :], batch, seq, math.gcd(seq, 256), dk, dv)
        n_new = n_new[:, :, 0, :]
        m_new = m_new[:, :, 0, 0]
    else:
        r = g_kv.shape[0]
        q_lat = _q_absorb(q_cat, w_uk_f).reshape(t, MLA_HEADS, r)
        q_rope = q_cat.reshape(t, MLA_HEADS, QK_SLOT)[:, :, MLA_NOPE:MLA_NOPE + MLA_ROPE].astype(F32)
        o_lat = _mla_decode(state["page_table"], q_lat, q_rope, ckv[:, None, :], kr[:, None, :],
                            state["cache_ckv"], state["cache_kr"], state["layer"], 16)
        o_mla = _o_absorb(o_lat.reshape(t, MLA_HEADS * r), w_uv_f)
        h_ml, c_new, n_new, m_new = _mlstm_sample(
            z[:, None, :], bias_r, g_ml[None, :], state["C"], state["n"], state["m"][:, None, :],
            dk, dv)
        h_ml = h_ml[:, 0, :]
        m_new = m_new[:, 0, :]

    x1 = _out_proj(o_mla, h_ml, w_out, x, min(t, 1024), 512)
    if prompt:
        tmf = 1024
        a, u_tail = _ffn_up(x1, g_ffn[None, :], w_gate, w_up, w_conv, b_conv[None, :], tmf, 256,
                            seq=seq)
        tiles = seq // tmf
        u_tail = u_tail.reshape(batch, tiles, 8, -1)
        conv_new = u_tail[:, tiles - 1, 8 - (CONV_W - 1):, :]
    else:
        conv0 = state["conv"]
        a, u = _ffn_up(x1, g_ffn[None, :], w_gate, w_up, w_conv, b_conv[None, :], t, 256,
                       hist=(conv0[:, 0, :], conv0[:, 1, :]))
        conv_new = jnp.stack([conv0[:, 1, :], u], axis=1)
    y = _ffn_down(a, w_down, x1, state["g_out"][None, :], min(t, 512), 512)
    return y, (ckv, kr, c_new, n_new, m_new, conv_new)


def kernel(x_prompt, x_sample, cache_ckv, cache_kr, state_C, state_n, state_m, state_conv, page_table, g_mix, w_in, g_q, w_uq, g_kv, w_uk, w_uv, b_i, b_f, g_ml, w_out, g_ffn, w_gate, w_up, w_conv, b_conv, w_down, g_final):
    depth = w_in.shape[0]
    assert depth == 1, "the final RMSNorm is fused into the (single) layer's FFN kernel"
    bp, sp, d = x_prompt.shape
    bs, ss, _ = x_sample.shape
    assert ss == 1, "the sample group decodes one token per sequence"
    past_len = page_table.shape[1] * cache_ckv.shape[2]
    pos_p = jnp.arange(sp)
    pos_s = past_len + jnp.arange(ss)
    l = 0
    lw = (g_mix[l], w_in[l], g_q[l], w_uq[l], g_kv[l], w_uk[l], w_uv[l], b_i[l], b_f[l],
          g_ml[l], w_out[l], g_ffn[l], w_gate[l], w_up[l], w_conv[l], b_conv[l], w_down[l])
    q_lora, kv_rank = g_q.shape[1], g_kv.shape[1]
    nh = ML_HEADS
    dk = (w_in.shape[2] - q_lora - kv_rank - MLA_ROPE - 2 * nh) // (6 * nh)
    prep = (_prep_w_in(w_in[l], q_lora, kv_rank, dk, 2 * dk), _prep_w_uq(w_uq[l]),
            w_uk[l].reshape(kv_rank, -1), w_uv[l].reshape(kv_rank, -1))

    yp, new_p = _layer(x_prompt.reshape(bp * sp, d), pos_p, lw, prep, "prompt",
                       dict(batch=bp, seq=sp, g_out=g_final))
    ys, new_s = _layer(x_sample.reshape(bs * ss, d), pos_s, lw, prep, "sample",
                       dict(page_table=page_table, cache_ckv=cache_ckv, cache_kr=cache_kr,
                            layer=l, C=state_C[l], n=state_n[l], m=state_m[l],
                            conv=state_conv[l], g_out=g_final))
    ckv_p, kr_p, c_p, n_p, m_p, conv_p = new_p
    ckv_s, kr_s, c_s, n_s, m_s, conv_s = new_s
    return (yp.reshape(bp, sp, d), ys.reshape(bs, ss, d),
            ckv_p.reshape(1, bp, sp, -1), kr_p.reshape(1, bp, sp, -1),
            c_p[None], n_p[None], m_p[None], conv_p[None],
            ckv_s.reshape(1, bs, ss, -1), kr_s.reshape(1, bs, ss, -1),
            c_s[None], n_s[None], m_s[None], conv_s[None])
```

```python
import functools
import math

import jax
import jax.numpy as jnp
from jax import lax
from jax.experimental import pallas as pl
from jax.experimental.pallas import tpu as pltpu

F32 = jnp.float32
BF16 = jnp.bfloat16

MLA_HEADS = 8
MLA_NOPE = 128
MLA_ROPE = 64
MLA_V = 128
ML_HEADS = 4
CONV_W = 3
ROPE_THETA = 10000.0
EPS = 1e-6
SCALE = (MLA_NOPE + MLA_ROPE) ** -0.5
LOG2E = math.log2(math.e)

LANES = 128
QK_SLOT = 2 * LANES
VMEM_LIMIT = 56 * 1024 * 1024


def _cparams(sem, vmem=VMEM_LIMIT):
    return pltpu.CompilerParams(dimension_semantics=sem, vmem_limit_bytes=vmem)


def _rms(x, g):
    return x * lax.rsqrt(jnp.mean(x * x, axis=-1, keepdims=True) + EPS) * g


def _sigmoid(x):
    return 1.0 / (1.0 + jnp.exp(-x))


def _log_sigmoid(x):
    return jnp.minimum(x, 0.0) - jnp.log1p(jnp.exp(-jnp.abs(x)))


def _dot(a, b):
    return jnp.dot(a, b, preferred_element_type=F32)


def _dot_nt(a, b):
    return lax.dot_general(a, b, (((1,), (1,)), ((), ())), preferred_element_type=F32)


def _dot_tn(a, b):
    return lax.dot_general(a, b, (((0,), (0,)), ((), ())), preferred_element_type=F32)


def _proj_in_kernel(x_ref, g_ref, w_ref, z_ref, xn_ref):
    @pl.when(pl.program_id(1) == 0)
    def _():
        xn_ref[...] = _rms(x_ref[...], g_ref[...]).astype(BF16)

    z_ref[...] = _dot(xn_ref[...], w_ref[...])


def _proj_in(x, g, w, tm, tn):
    t, d = x.shape
    n = w.shape[1]
    return pl.pallas_call(
        _proj_in_kernel,
        out_shape=jax.ShapeDtypeStruct((t, n), F32),
        grid=(t // tm, n // tn),
        in_specs=[pl.BlockSpec((tm, d), lambda i, j: (i, 0)),
                  pl.BlockSpec((1, d), lambda i, j: (0, 0)),
                  pl.BlockSpec((d, tn), lambda i, j: (0, j))],
        out_specs=pl.BlockSpec((tm, tn), lambda i, j: (i, j)),
        scratch_shapes=[pltpu.VMEM((tm, d), BF16)],
        compiler_params=_cparams(("arbitrary", "arbitrary")),
        name="proj_in",
    )(x, g, w)


def _rope128(x, cos, sin):
    return x * cos + pltpu.roll(x, 64, axis=1) * sin


def _mla_proj_kernel(cq_ref, ckv_ref, tail_ref, gq_ref, gkv_ref, wq_ref, wk_ref, wv_ref,
                     cos_ref, sin_ref, q_ref, k_ref, v_ref, ckv_out_ref, kr_out_ref):
    cos = cos_ref[...]
    sin = sin_ref[...]
    cqn = _rms(cq_ref[...], gq_ref[...]).astype(BF16)
    q = _dot(cqn, wq_ref[...]) * (SCALE * LOG2E)
    ckv = _rms(ckv_ref[...], gkv_ref[...])
    ckv_out_ref[...] = ckv
    ckv_b = ckv.astype(BF16)
    kn = _dot(ckv_b, wk_ref[...])
    v_ref[...] = _dot(ckv_b, wv_ref[...]).astype(BF16)
    kr = _rope128(tail_ref[...], cos, sin)
    kr_out_ref[...] = kr[:, :MLA_ROPE]
    kr_b = kr.astype(BF16)
    for h in range(MLA_HEADS):
        q0 = h * QK_SLOT
        q_ref[:, q0:q0 + LANES] = q[:, q0:q0 + LANES].astype(BF16)
        q_ref[:, q0 + LANES:q0 + QK_SLOT] = _rope128(
            q[:, q0 + LANES:q0 + QK_SLOT], cos, sin).astype(BF16)
        k_ref[:, q0:q0 + LANES] = kn[:, h * MLA_NOPE:(h + 1) * MLA_NOPE].astype(BF16)
        k_ref[:, q0 + LANES:q0 + QK_SLOT] = kr_b


def _mla_proj(z, g_q, g_kv, wq, wk, wv, cos, sin, tm):
    t = z.shape[0]
    rows = cos.shape[0] // tm
    hq = MLA_HEADS * QK_SLOT
    qlr, kvr = wq.shape[0], wk.shape[0]
    assert qlr == kvr, "c_q and c_kv are addressed as equal-width column blocks of z"
    tail_blk = (z.shape[1] - 2 * LANES) // LANES
    const = lambda i: (0, 0)
    return pl.pallas_call(
        _mla_proj_kernel,
        out_shape=(jax.ShapeDtypeStruct((t, hq), BF16),
                   jax.ShapeDtypeStruct((t, hq), BF16),
                   jax.ShapeDtypeStruct((t, MLA_HEADS * MLA_V), BF16),
                   jax.ShapeDtypeStruct((t, kvr), F32),
                   jax.ShapeDtypeStruct((t, MLA_ROPE), F32)),
        grid=(t // tm,),
        in_specs=[pl.BlockSpec((tm, qlr), lambda i: (i, 0)),
                  pl.BlockSpec((tm, kvr), lambda i: (i, 1)),
                  pl.BlockSpec((tm, LANES), lambda i: (i, tail_blk)),
                  pl.BlockSpec((1, qlr), const),
                  pl.BlockSpec((1, kvr), const),
                  pl.BlockSpec(wq.shape, const),
                  pl.BlockSpec(wk.shape, const),
                  pl.BlockSpec(wv.shape, const),
                  pl.BlockSpec((tm, LANES), lambda i: (i % rows, 0)),
                  pl.BlockSpec((tm, LANES), lambda i: (i % rows, 0))],
        out_specs=(pl.BlockSpec((tm, hq), lambda i: (i, 0)),
                   pl.BlockSpec((tm, hq), lambda i: (i, 0)),
                   pl.BlockSpec((tm, MLA_HEADS * MLA_V), lambda i: (i, 0)),
                   pl.BlockSpec((tm, kvr), lambda i: (i, 0)),
                   pl.BlockSpec((tm, MLA_ROPE), lambda i: (i, 0))),
        compiler_params=_cparams(("arbitrary",)),
        name="mla_proj",
    )(z, z, z, g_q, g_kv, wq, wk, wv, cos, sin)


def _flash_kernel(q_ref, k_ref, v_ref, o_ref, m_ref, l_ref, acc_ref, *, row_split):
    qi = pl.program_id(2)
    tq = q_ref.shape[0]
    sub = tq // row_split
    m_ref[...] = jnp.full(m_ref.shape, -jnp.inf, F32)
    l_ref[...] = jnp.zeros(l_ref.shape, F32)
    acc_ref[...] = jnp.zeros(acc_ref.shape, F32)

    def update(j, diagonal):
        off = pl.multiple_of(j * tq, tq)
        k = k_ref[pl.ds(off, tq), :]
        v = v_ref[pl.ds(off, tq), :]
        for r in range(row_split):
            rows = pl.ds(r * sub, sub)
            s = _dot_nt(q_ref[rows, :], k)
            if diagonal:
                row = lax.broadcasted_iota(jnp.int32, s.shape, 0) + r * sub
                col = lax.broadcasted_iota(jnp.int32, s.shape, 1)
                s = jnp.where(col <= row, s, -jnp.inf)
            m_prev = m_ref[rows, :]
            m_new = jnp.maximum(m_prev, jnp.max(s, axis=1, keepdims=True))
            alpha = jnp.exp2(m_prev - m_new)
            p = jnp.exp2(s - m_new)
            l_ref[rows, :] = alpha * l_ref[rows, :] + jnp.sum(p, axis=1, keepdims=True)
            acc_ref[rows, :] = alpha * acc_ref[rows, :] + _dot(p.astype(BF16), v)
            m_ref[rows, :] = m_new

    def body(j, carry):
        update(j, False)
        return carry

    lax.fori_loop(0, qi, body, 0)
    update(qi, True)
    o_ref[...] = (acc_ref[...] / l_ref[...]).astype(o_ref.dtype)


def _flash(q, k, v, batch, seq, tq):
    nb = seq // tq
    t = q.shape[0]
    return pl.pallas_call(
        functools.partial(_flash_kernel, row_split=1),
        out_shape=jax.ShapeDtypeStruct((t, MLA_HEADS * MLA_V), BF16),
        grid=(batch, MLA_HEADS, nb),
        in_specs=[pl.BlockSpec((tq, QK_SLOT), lambda b, h, i: (b * nb + i, h)),
                  pl.BlockSpec((seq, QK_SLOT), lambda b, h, i: (b, h)),
                  pl.BlockSpec((seq, MLA_V), lambda b, h, i: (b, h))],
        out_specs=pl.BlockSpec((tq, MLA_V), lambda b, h, i: (b * nb + i, h)),
        scratch_shapes=[pltpu.VMEM((tq, 1), F32), pltpu.VMEM((tq, 1), F32),
                        pltpu.VMEM((tq, MLA_V), F32)],
        compiler_params=_cparams(("arbitrary",) * 3),
        name="flash_attn",
    )(q, k, v)


def _mlstm_chunk_kernel(q_ref, k_ref, v_ref, o_ref, gc_ref, gr_ref, bias_r_ref, bias_c_ref,
                        gml_ref, h_ref, c_out_ref, n_out_ref, m_out_ref,
                        c_sc, n_sc, m_sc, *, dk, dv):
    c_idx = pl.program_id(1)
    L = q_ref.shape[0]
    nh = ML_HEADS

    @pl.when(c_idx == 0)
    def _():
        c_sc[...] = jnp.zeros(c_sc.shape, F32)
        n_sc[...] = jnp.zeros(n_sc.shape, F32)
        m_sc[...] = jnp.zeros(m_sc.shape, F32)

    gates_c = gc_ref[...][:, :2 * nh] + bias_r_ref[...]
    gates_r = gr_ref[...] + bias_c_ref[...]
    row = lax.broadcasted_iota(jnp.int32, (L, L), 0)
    col = lax.broadcasted_iota(jnp.int32, (L, L), 1)
    lower = col <= row
    lower_f = lower.astype(F32)
    upper_f = (row <= col).astype(F32)
    hp = lax.Precision.HIGHEST
    b_col = jnp.dot(lower_f, _log_sigmoid(gates_c[:, nh:]), precision=hp,
                    preferred_element_type=F32)
    b_row = jnp.dot(_log_sigmoid(gates_r[nh:, :]), upper_f, precision=hp,
                    preferred_element_type=F32)

    for h in range(nh):
        bc = b_col[:, h:h + 1]
        br = b_row[h:h + 1, :]
        ic = gates_c[:, h:h + 1]
        ir = gates_r[h:h + 1, :]
        m_prev = m_sc[h][0:1, 0:1]
        log_d = jnp.where(lower, bc - br + ir, -jnp.inf)
        log_prev = bc + m_prev
        m_t = jnp.maximum(log_prev, jnp.max(log_d, axis=1, keepdims=True))
        q = q_ref[:, h * dk:(h + 1) * dk]
        k = k_ref[:, h * dk:(h + 1) * dk] * (dk ** -0.5)
        v = v_ref[:, h * dv:(h + 1) * dv]
        qb = q.astype(BF16)
        kb = k.astype(BF16)
        s = _dot_nt(qb, kb) * jnp.exp(log_d - m_t)
        w_prev = jnp.exp(log_prev - m_t)
        c_prev = c_sc[h]
        n_prev = n_sc[h][0:1, :]
        num = _dot(s.astype(BF16), v.astype(BF16)) + w_prev * _dot_nt(qb, c_prev.astype(BF16))
        den = jnp.sum(s, axis=1, keepdims=True) + w_prev * jnp.sum(q * n_prev, axis=1, keepdims=True)
        hh = num / jnp.maximum(jnp.abs(den), jnp.exp(-m_t))
        hh = _rms(hh, gml_ref[:, h * dv:(h + 1) * dv])
        h_ref[:, h * dv:(h + 1) * dv] = (hh * _sigmoid(o_ref[:, h * dv:(h + 1) * dv])).astype(h_ref.dtype)
        m_new = m_t[L - 1:L, :]
        b_last = bc[L - 1:L, :]
        w_s = jnp.exp(b_last - bc + ic - m_new)
        decay = jnp.exp(b_last + m_prev - m_new)
        c_sc[h] = decay * c_prev + _dot_tn((w_s * v).astype(BF16), kb)
        n_sc[h] = jnp.broadcast_to(decay * n_prev + jnp.sum(w_s * k, axis=0, keepdims=True),
                                   n_sc.shape[1:])
        m_sc[h] = jnp.broadcast_to(m_new, m_sc.shape[1:])

    @pl.when(c_idx == pl.num_programs(1) - 1)
    def _():
        c_out_ref[0] = c_sc[...]
        n_out_ref[0] = n_sc[...]
        m_out_ref[0] = m_sc[...]


def _mlstm_prompt(z, gates_row, bias_r, bias_c, g_ml, batch, seq, chunk, dk, dv):
    t = z.shape[0]
    nc = seq // chunk
    nh = ML_HEADS
    kern = functools.partial(_mlstm_chunk_kernel, dk=dk, dv=dv)
    tok = lambda w, blk: pl.BlockSpec((chunk, w), lambda b, c: (b * nc + c, blk))
    const = lambda b, c: (0, 0)
    st = lambda b, c: (b, 0, 0, 0)
    return pl.pallas_call(
        kern,
        out_shape=(jax.ShapeDtypeStruct((t, nh * dv), BF16),
                   jax.ShapeDtypeStruct((batch, nh, dv, dk), F32),
                   jax.ShapeDtypeStruct((batch, nh, 8, dk), F32),
                   jax.ShapeDtypeStruct((batch, nh, 8, LANES), F32)),
        grid=(batch, nc),
        in_specs=[tok(nh * dk, 2), tok(nh * dk, 3), tok(nh * dv, 2), tok(nh * dv, 3),
                  tok(LANES, 33),
                  pl.BlockSpec((2 * nh, chunk), lambda b, c: (0, b * nc + c)),
                  pl.BlockSpec((1, 2 * nh), const),
                  pl.BlockSpec((2 * nh, 1), const),
                  pl.BlockSpec((1, nh * dv), const)],
        out_specs=(pl.BlockSpec((chunk, nh * dv), lambda b, c: (b * nc + c, 0)),
                   pl.BlockSpec((1, nh, dv, dk), st),
                   pl.BlockSpec((1, nh, 8, dk), st),
                   pl.BlockSpec((1, nh, 8, LANES), st)),
        scratch_shapes=[pltpu.VMEM((nh, dv, dk), F32), pltpu.VMEM((nh, 8, dk), F32),
                        pltpu.VMEM((nh, 8, LANES), F32)],
        compiler_params=_cparams(("arbitrary", "arbitrary")),
        name="mlstm_chunk",
    )(z, z, z, z, z, gates_row, bias_r, bias_c, g_ml)


def _mlstm_step_kernel(q_ref, k_ref, v_ref, o_ref, g_ref, bias_ref, gml_ref, c0_ref, n0_ref,
                       m0_ref, h_ref, c_ref, n_ref, m_ref, *, dk, dv):
    nh = ML_HEADS
    gates = g_ref[0][:, :2 * nh] + bias_ref[...]
    m_all = m0_ref[0]
    n_all = n0_ref[0]
    m_cols = []
    for h in range(nh):
        i_g = gates[:, h:h + 1]
        b = _log_sigmoid(gates[:, nh + h:nh + h + 1])
        m_prev = m_all[:, h:h + 1]
        q = q_ref[0][:, h * dk:(h + 1) * dk]
        k = k_ref[0][:, h * dk:(h + 1) * dk] * (dk ** -0.5)
        v = v_ref[0][:, h * dv:(h + 1) * dv]
        qb = q.astype(BF16)
        kb = k.astype(BF16)
        log_prev = b + m_prev
        m_t = jnp.maximum(log_prev, i_g)
        qk = jnp.sum(qb.astype(F32) * kb.astype(F32), axis=1, keepdims=True)
        s = qk * jnp.exp(i_g - m_t)
        w_prev = jnp.exp(log_prev - m_t)
        c_prev = c0_ref[0, h]
        n_prev = n_all[h:h + 1, :]
        q8 = jnp.broadcast_to(qb, (8, dk))
        cq = _dot_nt(q8, c_prev.astype(BF16))[0:1, :]
        num = s * v.astype(BF16).astype(F32) + w_prev * cq
        den = s + w_prev * jnp.sum(q * n_prev, axis=1, keepdims=True)
        hh = num / jnp.maximum(jnp.abs(den), jnp.exp(-m_t))
        hh = _rms(hh, gml_ref[:, h * dv:(h + 1) * dv])
        h_ref[0, :, h * dv:(h + 1) * dv] = (hh * _sigmoid(o_ref[0][:, h * dv:(h + 1) * dv])).astype(h_ref.dtype)
        w_s = jnp.exp(i_g - m_t)
        decay = jnp.exp(log_prev - m_t)
        rows = lax.broadcasted_iota(jnp.int32, (8, dv), 0)
        wv8 = jnp.where(rows == 0, jnp.broadcast_to(w_s * v, (8, dv)), 0.0).astype(BF16)
        k8 = jnp.broadcast_to(kb, (8, dk))
        c_ref[0, h] = decay * c_prev + _dot_tn(wv8, k8)
        n_ref[0, h:h + 1, :] = decay * n_prev + w_s * k
        m_cols.append(m_t)
    m_ref[0] = jnp.concatenate(m_cols, axis=1)


def _mlstm_sample(z3, bias_r, g_ml, c0, n0, m0, dk, dv):
    nb = z3.shape[0]
    nh = ML_HEADS
    kern = functools.partial(_mlstm_step_kernel, dk=dk, dv=dv)
    tok = lambda w, blk: pl.BlockSpec((1, 1, w), lambda b: (b, 0, blk))
    const = lambda b: (0, 0)
    return pl.pallas_call(
        kern,
        out_shape=(jax.ShapeDtypeStruct((nb, 1, nh * dv), BF16),
                   jax.ShapeDtypeStruct((nb, nh, dv, dk), F32),
                   jax.ShapeDtypeStruct((nb, nh, dk), F32),
                   jax.ShapeDtypeStruct((nb, 1, nh), F32)),
        grid=(nb,),
        in_specs=[tok(nh * dk, 2), tok(nh * dk, 3), tok(nh * dv, 2), tok(nh * dv, 3),
                  tok(LANES, 33),
                  pl.BlockSpec((1, 2 * nh), const),
                  pl.BlockSpec((1, nh * dv), const),
                  pl.BlockSpec((1, nh, dv, dk), lambda b: (b, 0, 0, 0)),
                  pl.BlockSpec((1, nh, dk), lambda b: (b, 0, 0)),
                  pl.BlockSpec((1, 1, nh), lambda b: (b, 0, 0))],
        out_specs=(pl.BlockSpec((1, 1, nh * dv), lambda b: (b, 0, 0)),
                   pl.BlockSpec((1, nh, dv, dk), lambda b: (b, 0, 0, 0)),
                   pl.BlockSpec((1, nh, dk), lambda b: (b, 0, 0)),
                   pl.BlockSpec((1, 1, nh), lambda b: (b, 0, 0))),
        compiler_params=_cparams(("arbitrary",)),
        name="mlstm_step",
    )(z3, z3, z3, z3, z3, bias_r, g_ml, c0, n0, m0)


def _q_absorb_kernel(q_ref, wk_ref, o_ref):
    o_ref[...] = _dot_nt(q_ref[:, :MLA_NOPE], wk_ref[...].astype(BF16))


def _q_absorb(q_cat, w_uk_flat):
    t = q_cat.shape[0]
    r = w_uk_flat.shape[0]
    return pl.pallas_call(
        _q_absorb_kernel,
        out_shape=jax.ShapeDtypeStruct((t, MLA_HEADS * r), F32),
        grid=(MLA_HEADS,),
        in_specs=[pl.BlockSpec((t, QK_SLOT), lambda h: (0, h)),
                  pl.BlockSpec((r, MLA_NOPE), lambda h: (0, h))],
        out_specs=pl.BlockSpec((t, r), lambda h: (0, h)),
        compiler_params=_cparams(("arbitrary",)),
        name="q_absorb",
    )(q_cat, w_uk_flat)


def _o_absorb_kernel(o_ref, wv_ref, out_ref):
    out_ref[...] = _dot(o_ref[...].astype(BF16), wv_ref[...].astype(BF16)).astype(out_ref.dtype)


def _o_absorb(o_lat2d, w_uv_flat):
    t = o_lat2d.shape[0]
    r = w_uv_flat.shape[0]
    return pl.pallas_call(
        _o_absorb_kernel,
        out_shape=jax.ShapeDtypeStruct((t, MLA_HEADS * MLA_V), BF16),
        grid=(MLA_HEADS,),
        in_specs=[pl.BlockSpec((t, r), lambda h: (0, h)),
                  pl.BlockSpec((r, MLA_V), lambda h: (0, h))],
        out_specs=pl.BlockSpec((t, MLA_V), lambda h: (0, h)),
        compiler_params=_cparams(("arbitrary",)),
        name="o_absorb",
    )(o_lat2d, w_uv_flat)


def _mla_decode_kernel(pt_ref, ql_ref, qr_ref, cn_ref, kn_ref, cc_hbm, ckt_hbm, o_ref,
                       cbuf, kbuf, sem, *, layer, n_seq, n_chunks, pages_per_chunk, page_size,
                       group, n_slots):
    total = (n_seq // group) * n_chunks
    pages_per_seq = n_chunks * pages_per_chunk
    kc = pages_per_chunk * page_size
    ahead = n_slots - 1

    def chunk_copies(step, slot):
        g = step // n_chunks
        c = step - g * n_chunks
        cps = []
        for a in range(group):
            base = (g * group + a) * pages_per_seq + c * pages_per_chunk
            for j in range(pages_per_chunk):
                page = pt_ref[base + j]
                dst = pl.ds(a * kc + j * page_size, page_size)
                cps.append(pltpu.make_async_copy(
                    cc_hbm.at[layer, page], cbuf.at[slot, dst], sem.at[0, slot]))
                cps.append(pltpu.make_async_copy(
                    ckt_hbm.at[layer, page], kbuf.at[slot, :, dst], sem.at[1, slot]))
        return cps

    def issue(step):
        if isinstance(step, int):
            src, slot = min(step, total - 1), step % n_slots
        else:
            src, slot = jnp.minimum(step, total - 1), lax.rem(step, n_slots)
        for cp in chunk_copies(src, slot):
            cp.start()

    for step in range(ahead):
        issue(step)

    def group_body(g, _):
        ql = [ql_ref[g * group + a] for a in range(group)]
        qr = [qr_ref[g * group + a] for a in range(group)]
        ql_b = [x.astype(BF16) for x in ql]
        qr_b = [x.astype(BF16) for x in qr]

        def chunk_body(c, carry):
            step = g * n_chunks + c
            slot = lax.rem(step, n_slots)
            issue(step + ahead)
            for cp in chunk_copies(step, slot):
                cp.wait()
            out = []
            for a in range(group):
                m, l, acc = carry[a]
                keys = pl.ds(a * kc, kc)
                cb = cbuf[slot, keys, :].astype(BF16)
                kb = kbuf[slot, :, keys].astype(BF16)
                s = _dot_nt(ql_b[a], cb) + _dot(qr_b[a], kb)
                m_new = jnp.maximum(m, jnp.max(s, axis=1, keepdims=True))
                alpha = jnp.exp2(m - m_new)
                p = jnp.exp2(s - m_new)
                l = alpha * l + jnp.sum(p, axis=1, keepdims=True)
                acc = alpha * acc + _dot(p.astype(BF16), cb)
                out.append((m_new, l, acc))
            return tuple(out)

        nh = ql[0].shape[0]
        init = tuple((jnp.full((nh, 1), -jnp.inf, F32), jnp.zeros((nh, 1), F32),
                      jnp.zeros(ql[0].shape, F32)) for _ in range(group))
        fin = lax.fori_loop(0, n_chunks, chunk_body, init)
        for a in range(group):
            m, l, acc = fin[a]
            b = g * group + a
            cn = cn_ref[b]
            kn = kn_ref[b]
            s_new = (jnp.sum(ql[a] * cn, axis=1, keepdims=True)
                     + jnp.sum(qr[a] * kn, axis=1, keepdims=True))
            m_f = jnp.maximum(m, s_new)
            alpha = jnp.exp2(m - m_f)
            p_new = jnp.exp2(s_new - m_f)
            o_ref[b] = (alpha * acc + p_new * cn) / (alpha * l + p_new)
        return 0

    lax.fori_loop(0, n_seq // group, group_body, 0)
    for step in range(total, total + ahead):
        for cp in chunk_copies(total - 1, step % n_slots):
            cp.wait()


def _mla_decode(page_table, q_lat, q_rope, ckv_new, kr_new, cache_ckv, cache_kr_t, layer,
                pages_per_chunk, group=2, n_slots=3):
    n_seq, n_pages = page_table.shape
    page_size, r = cache_ckv.shape[2], cache_ckv.shape[3]
    rope = cache_kr_t.shape[2]
    assert n_seq % group == 0 and n_pages % pages_per_chunk == 0
    n_chunks = n_pages // pages_per_chunk
    kc = pages_per_chunk * page_size
    kern = functools.partial(_mla_decode_kernel, layer=layer, n_seq=n_seq, n_chunks=n_chunks,
                             pages_per_chunk=pages_per_chunk, page_size=page_size,
                             group=group, n_slots=n_slots)
    vmem = pl.BlockSpec(memory_space=pltpu.VMEM)
    return pl.pallas_call(
        kern,
        out_shape=jax.ShapeDtypeStruct(q_lat.shape, F32),
        in_specs=[pl.BlockSpec(memory_space=pltpu.SMEM), vmem, vmem, vmem, vmem,
                  pl.BlockSpec(memory_space=pl.ANY), pl.BlockSpec(memory_space=pl.ANY)],
        out_specs=vmem,
        scratch_shapes=[pltpu.VMEM((n_slots, group * kc, r), F32),
                        pltpu.VMEM((n_slots, rope, group * kc), F32),
                        pltpu.SemaphoreType.DMA((2, n_slots))],
        compiler_params=pltpu.CompilerParams(vmem_limit_bytes=VMEM_LIMIT),
        name="mla_decode",
    )(page_table.reshape(-1), q_lat, q_rope, ckv_new, kr_new, cache_ckv, cache_kr_t)


def _out_proj_kernel(a_ref, h_ref, w1_ref, w2_ref, x_ref, o_ref, *, tn):
    a = a_ref[...]
    h = h_ref[...]
    for c in range(o_ref.shape[1] // tn):
        cols = pl.ds(c * tn, tn)
        o_ref[:, cols] = x_ref[:, cols] + _dot(a, w1_ref[:, cols]) + _dot(h, w2_ref[:, cols])


def _out_proj(a, h, w_out, x, tm, tn):
    t, d = x.shape
    ka = a.shape[1]
    kh = h.shape[1]
    assert ka == kh
    return pl.pallas_call(
        functools.partial(_out_proj_kernel, tn=tn),
        out_shape=jax.ShapeDtypeStruct((t, d), F32),
        grid=(t // tm,),
        in_specs=[pl.BlockSpec((tm, ka), lambda i: (i, 0)),
                  pl.BlockSpec((tm, kh), lambda i: (i, 0)),
                  pl.BlockSpec((ka, d), lambda i: (0, 0)),
                  pl.BlockSpec((kh, d), lambda i: (1, 0)),
                  pl.BlockSpec((tm, d), lambda i: (i, 0))],
        out_specs=pl.BlockSpec((tm, d), lambda i: (i, 0)),
        compiler_params=_cparams(("arbitrary",)),
        name="out_proj",
    )(a, h, w_out, w_out, x)


def _ffn_up_kernel(*refs, seq_tiles, decode, row_split):
    if decode:
        (x_ref, g_ref, wg_ref, wu_ref, wc_ref, bc_ref, h0_ref, h1_ref,
         a_ref, u_ref, hn_ref) = refs
    else:
        (x_ref, g_ref, wg_ref, wu_ref, wc_ref, bc_ref,
         a_ref, u_ref, hn_ref, prev_ref) = refs
    i = pl.program_id(0)
    f = pl.program_id(1)

    @pl.when(f == 0)
    def _():
        hn_ref[...] = _rms(x_ref[...], g_ref[...]).astype(BF16)

    wg = wg_ref[...].astype(BF16)
    wu = wu_ref[...].astype(BF16)
    wc = wc_ref[...]
    bc = bc_ref[...]
    tm = hn_ref.shape[0]
    tail = u_ref.shape[0]
    if decode:
        u = _dot(hn_ref[...], wg)
        up = _dot(hn_ref[...], wu)
        u_ref[...] = u
        conv = h0_ref[...] * wc[0:1, :] + h1_ref[...] * wc[1:2, :] + u * wc[2:3, :] + bc
        a_ref[...] = (conv * _sigmoid(conv) * up).astype(a_ref.dtype)
        return

    @pl.when(i == 0)
    def _():
        prev_ref[f] = jnp.zeros(prev_ref.shape[1:], F32)

    prev = jnp.where(lax.rem(i, seq_tiles) != 0, prev_ref[f], 0.0)
    sub = tm // row_split
    for r in range(row_split):
        rows = pl.ds(r * sub, sub)
        hn = hn_ref[rows, :]
        u = _dot(hn, wg)
        up = _dot(hn, wu)
        row = lax.broadcasted_iota(jnp.int32, u.shape, 0)
        p6 = jnp.broadcast_to(prev[6:7, :], u.shape)
        p7 = jnp.broadcast_to(prev[7:8, :], u.shape)
        u_m1 = jnp.where(row == 0, p7, pltpu.roll(u, 1, axis=0))
        u_m2 = jnp.where(row == 0, p6, jnp.where(row == 1, p7, pltpu.roll(u, 2, axis=0)))
        conv = u_m2 * wc[0:1, :] + u_m1 * wc[1:2, :] + u * wc[2:3, :] + bc
        a_ref[rows, :] = (conv * _sigmoid(conv) * up).astype(a_ref.dtype)
        prev = u[sub - 8:, :]
    prev_ref[f] = prev
    u_ref[...] = prev[8 - tail:, :]


def _ffn_up(x, g, w_gate, w_up, w_conv, b_conv, tm, tf, seq=None, hist=None):
    t, d = x.shape
    ff = w_gate.shape[1]
    decode = hist is not None
    tail = tm if decode else 8
    nf = ff // tf
    kern = functools.partial(_ffn_up_kernel, seq_tiles=None if decode else seq // tm,
                             decode=decode, row_split=4)
    in_specs = [pl.BlockSpec((tm, d), lambda i, f: (i, 0)),
                pl.BlockSpec((1, d), lambda i, f: (0, 0)),
                pl.BlockSpec((d, tf), lambda i, f: (0, f)),
                pl.BlockSpec((d, tf), lambda i, f: (0, f)),
                pl.BlockSpec((CONV_W, tf), lambda i, f: (0, f)),
                pl.BlockSpec((1, tf), lambda i, f: (0, f))]
    args = [x, g, w_gate, w_up, w_conv, b_conv]
    scratch = [pltpu.VMEM((tm, d), BF16)]
    if decode:
        in_specs += [pl.BlockSpec((tm, tf), lambda i, f: (i, f))] * 2
        args += list(hist)
    else:
        scratch.append(pltpu.VMEM((nf, 8, tf), F32))
    return pl.pallas_call(
        kern,
        out_shape=(jax.ShapeDtypeStruct((t, ff), BF16),
                   jax.ShapeDtypeStruct((t // tm * tail, ff), F32)),
        grid=(t // tm, nf),
        in_specs=in_specs,
        out_specs=(pl.BlockSpec((tm, tf), lambda i, f: (i, f)),
                   pl.BlockSpec((tail, tf), lambda i, f: (i, f))),
        scratch_shapes=scratch,
        compiler_params=_cparams(("arbitrary", "arbitrary")),
        name="ffn_up",
    )(*args)


def _ffn_down_kernel(a_ref, w_ref, x_ref, g_ref, y_ref, *, tn):
    k = pl.program_id(1)

    @pl.when(k == 0)
    def _():
        y_ref[...] = x_ref[...]

    a = a_ref[...]
    for c in range(y_ref.shape[1] // tn):
        cols = pl.ds(c * tn, tn)
        y_ref[:, cols] += _dot(a, w_ref[:, cols].astype(BF16))

    @pl.when(k == pl.num_programs(1) - 1)
    def _():
        y_ref[...] = _rms(y_ref[...], g_ref[...])


def _ffn_down(a, w_down, x, g_final, tm, tk, tn):
    t, d = x.shape
    ff = a.shape[1]
    return pl.pallas_call(
        functools.partial(_ffn_down_kernel, tn=tn),
        out_shape=jax.ShapeDtypeStruct((t, d), F32),
        grid=(t // tm, ff // tk),
        in_specs=[pl.BlockSpec((tm, tk), lambda i, k: (i, k)),
                  pl.BlockSpec((tk, d), lambda i, k: (k, 0)),
                  pl.BlockSpec((tm, d), lambda i, k: (i, 0)),
                  pl.BlockSpec((1, d), lambda i, k: (0, 0))],
        out_specs=pl.BlockSpec((tm, d), lambda i, k: (i, 0)),
        compiler_params=_cparams(("arbitrary", "arbitrary")),
        name="ffn_down",
    )(a, w_down, x, g_final)


def _rope_tables(pos):
    inv = ROPE_THETA ** (-jnp.arange(0, MLA_ROPE, 2, dtype=F32) / MLA_ROPE)
    ang = pos.astype(F32)[:, None] * inv[None, :]
    cos, sin = jnp.cos(ang), jnp.sin(ang)
    zero = jnp.zeros_like(cos)
    return (jnp.concatenate([cos, cos, zero, zero], axis=1),
            jnp.concatenate([-sin, sin, zero, zero], axis=1))


def _prep_w_in(w_in, q_lora, kv_rank, dk, dv):
    nh = ML_HEADS
    o_kr = q_lora + kv_rank
    o_qm = o_kr + MLA_ROPE
    o_gate = o_qm + 2 * nh * dk + 2 * nh * dv
    half = MLA_ROPE // 2
    pad = jnp.zeros((w_in.shape[0], LANES - 2 * nh), w_in.dtype)
    cols = [w_in[:, :o_kr], w_in[:, o_qm:o_gate],
            w_in[:, o_kr:o_qm], w_in[:, o_kr + half:o_qm], w_in[:, o_kr:o_kr + half],
            w_in[:, o_gate:], pad]
    return jnp.concatenate(cols, axis=1).astype(BF16)


def _prep_w_uq(w_uq):
    hd = MLA_NOPE + MLA_ROPE
    half = MLA_ROPE // 2
    cols = []
    for h in range(MLA_HEADS):
        o = h * hd
        cols += [w_uq[:, o:o + hd], w_uq[:, o + MLA_NOPE + half:o + hd],
                 w_uq[:, o + MLA_NOPE:o + MLA_NOPE + half]]
    return jnp.concatenate(cols, axis=1).astype(BF16)


def _layer(x, pos, lw, prep, mode, state):
    (g_mix, w_in, g_q, w_uq, g_kv, w_uk, w_uv, b_i, b_f, g_ml, w_out,
     g_ffn, w_gate, w_up, w_conv, b_conv, w_down) = lw
    w_in_p, w_uq_p, w_uk_f, w_uv_f, w_out_b = prep
    t, d = x.shape
    nh = ML_HEADS
    dk = (w_in.shape[1] - g_q.shape[0] - g_kv.shape[0] - MLA_ROPE - 2 * nh) // (6 * nh)
    dv = 2 * dk
    prompt = mode == "prompt"
    tm = 512 if prompt else t
    cos, sin = _rope_tables(pos)
    if not prompt:
        cos = jnp.broadcast_to(cos, (t, LANES))
        sin = jnp.broadcast_to(sin, (t, LANES))

    z = _proj_in(x, g_mix[None, :], w_in_p, tm, w_in_p.shape[1] // 2)
    q_cat, k_cat, v, ckv, kr = _mla_proj(
        z, g_q[None, :], g_kv[None, :], w_uq_p, w_uk_f.astype(BF16), w_uv_f.astype(BF16),
        cos, sin, tm)
    bias_r = jnp.concatenate([b_i, b_f])[None, :].astype(F32)

    if prompt:
        batch, seq = state["batch"], state["seq"]
        o_mla = _flash(q_cat, k_cat, v, batch, seq, min(seq, 1024))
        gates_row = z[:, 33 * LANES:33 * LANES + 2 * nh].T
        h_ml, c_new, n_new, m_new = _mlstm_prompt(
            z, gates_row, bias_r, bias_r.T, g_ml[None, :], batch, seq, math.gcd(seq, 256), dk, dv)
        n_new = n_new[:, :, 0, :]
        m_new = m_new[:, :, 0, 0]
    else:
        r = g_kv.shape[0]
        q_lat = _q_absorb(q_cat, w_uk_f).reshape(t, MLA_HEADS, r)
        q_rope = q_cat.reshape(t, MLA_HEADS, QK_SLOT)[:, :, MLA_NOPE:MLA_NOPE + MLA_ROPE].astype(F32)
        o_lat = _mla_decode(state["page_table"], q_lat, q_rope, ckv[:, None, :], kr[:, None, :],
                            state["cache_ckv"], state["cache_kr_t"], state["layer"], 16)
        o_mla = _o_absorb(o_lat.reshape(t, MLA_HEADS * r), w_uv_f)
        h_ml, c_new, n_new, m_new = _mlstm_sample(
            z[:, None, :], bias_r, g_ml[None, :], state["C"], state["n"], state["m"][:, None, :],
            dk, dv)
        h_ml = h_ml[:, 0, :]
        m_new = m_new[:, 0, :]

    x1 = _out_proj(o_mla, h_ml, w_out_b, x, min(t, 512), 512)
    if prompt:
        tmf = 1024
        a, u_tail = _ffn_up(x1, g_ffn[None, :], w_gate, w_up, w_conv, b_conv[None, :], tmf, 256,
                            seq=seq)
        tiles = seq // tmf
        u_tail = u_tail.reshape(batch, tiles, 8, -1)
        conv_new = u_tail[:, tiles - 1, 8 - (CONV_W - 1):, :]
    else:
        conv0 = state["conv"]
        a, u = _ffn_up(x1, g_ffn[None, :], w_gate, w_up, w_conv, b_conv[None, :], t, 256,
                       hist=(conv0[:, 0, :], conv0[:, 1, :]))
        conv_new = jnp.stack([conv0[:, 1, :], u], axis=1)
    y = _ffn_down(a, w_down, x1, state["g_out"][None, :], min(t, 1024), 512, 512)
    return y, (ckv, kr, c_new, n_new, m_new, conv_new)


def kernel(x_prompt, x_sample, cache_ckv, cache_kr, state_C, state_n, state_m, state_conv, page_table, g_mix, w_in, g_q, w_uq, g_kv, w_uk, w_uv, b_i, b_f, g_ml, w_out, g_ffn, w_gate, w_up, w_conv, b_conv, w_down, g_final):
    depth = w_in.shape[0]
    assert depth == 1, "the final RMSNorm is fused into the (single) layer's FFN kernel"
    bp, sp, d = x_prompt.shape
    bs, ss, _ = x_sample.shape
    assert ss == 1, "the sample group decodes one token per sequence"
    past_len = page_table.shape[1] * cache_ckv.shape[2]
    pos_p = jnp.arange(sp)
    pos_s = past_len + jnp.arange(ss)
    l = 0
    lw = (g_mix[l], w_in[l], g_q[l], w_uq[l], g_kv[l], w_uk[l], w_uv[l], b_i[l], b_f[l],
          g_ml[l], w_out[l], g_ffn[l], w_gate[l], w_up[l], w_conv[l], b_conv[l], w_down[l])
    q_lora, kv_rank = g_q.shape[1], g_kv.shape[1]
    nh = ML_HEADS
    dk = (w_in.shape[2] - q_lora - kv_rank - MLA_ROPE - 2 * nh) // (6 * nh)
    prep = (_prep_w_in(w_in[l], q_lora, kv_rank, dk, 2 * dk), _prep_w_uq(w_uq[l]),
            w_uk[l].reshape(kv_rank, -1), w_uv[l].reshape(kv_rank, -1), w_out[l].astype(BF16))
    cache_kr_t = jnp.swapaxes(cache_kr, 2, 3)

    yp, new_p = _layer(x_prompt.reshape(bp * sp, d), pos_p, lw, prep, "prompt",
                       dict(batch=bp, seq=sp, g_out=g_final))
    ys, new_s = _layer(x_sample.reshape(bs * ss, d), pos_s, lw, prep, "sample",
                       dict(page_table=page_table, cache_ckv=cache_ckv, cache_kr_t=cache_kr_t,
                            layer=l, C=state_C[l], n=state_n[l], m=state_m[l],
                            conv=state_conv[l], g_out=g_final))
    ckv_p, kr_p, c_p, n_p, m_p, conv_p = new_p
    ckv_s, kr_s, c_s, n_s, m_s, conv_s = new_s
    return (yp.reshape(bp, sp, d), ys.reshape(bs, ss, d),
            ckv_p.reshape(1, bp, sp, -1), kr_p.reshape(1, bp, sp, -1),
            c_p[None], n_p[None], m_p[None], conv_p[None],
            ckv_s.reshape(1, bs, ss, -1), kr_s.reshape(1, bs, ss, -1),
            c_s[None], n_s[None], m_s[None], conv_s[None])
```

```python
import functools
import math

import jax
import jax.numpy as jnp
from jax import lax
from jax.experimental import pallas as pl
from jax.experimental.pallas import tpu as pltpu

F32 = jnp.float32
BF16 = jnp.bfloat16

MLA_HEADS = 8
MLA_NOPE = 128
MLA_ROPE = 64
MLA_V = 128
ML_HEADS = 4
CONV_W = 3
ROPE_THETA = 10000.0
EPS = 1e-6
SCALE = (MLA_NOPE + MLA_ROPE) ** -0.5
LOG2E = math.log2(math.e)

LANES = 128
QK_SLOT = 2 * LANES
VMEM_LIMIT = 56 * 1024 * 1024


def _cparams(sem, vmem=VMEM_LIMIT):
    return pltpu.CompilerParams(dimension_semantics=sem, vmem_limit_bytes=vmem)


def _rms(x, g):
    return x * lax.rsqrt(jnp.mean(x * x, axis=-1, keepdims=True) + EPS) * g


def _sigmoid(x):
    return 1.0 / (1.0 + jnp.exp(-x))


def _log_sigmoid(x):
    return jnp.minimum(x, 0.0) - jnp.log1p(jnp.exp(-jnp.abs(x)))


def _dot(a, b):
    return jnp.dot(a, b, preferred_element_type=F32)


def _dot_nt(a, b):
    return lax.dot_general(a, b, (((1,), (1,)), ((), ())), preferred_element_type=F32)


def _dot_tn(a, b):
    return lax.dot_general(a, b, (((0,), (0,)), ((), ())), preferred_element_type=F32)


def _proj_in_kernel(x_ref, g_ref, w_ref, z_ref, xn_ref):
    @pl.when(pl.program_id(1) == 0)
    def _():
        xn_ref[...] = _rms(x_ref[...], g_ref[...]).astype(BF16)

    z_ref[...] = _dot_nt(xn_ref[...], w_ref[...])


def _proj_in(x, g, w_t, tm, tn):
    t, d = x.shape
    n = w_t.shape[0]
    return pl.pallas_call(
        _proj_in_kernel,
        out_shape=jax.ShapeDtypeStruct((t, n), F32),
        grid=(t // tm, n // tn),
        in_specs=[pl.BlockSpec((tm, d), lambda i, j: (i, 0)),
                  pl.BlockSpec((1, d), lambda i, j: (0, 0)),
                  pl.BlockSpec((tn, d), lambda i, j: (j, 0))],
        out_specs=pl.BlockSpec((tm, tn), lambda i, j: (i, j)),
        scratch_shapes=[pltpu.VMEM((tm, d), BF16)],
        compiler_params=_cparams(("arbitrary", "arbitrary")),
        name="proj_in",
    )(x, g, w_t)


def _rope128(x, cos, sin):
    return x * cos + pltpu.roll(x, 64, axis=1) * sin


def _mla_proj_kernel(cq_ref, ckv_ref, tail_ref, gq_ref, gkv_ref, wq_ref, wk_ref, wv_ref,
                     cos_ref, sin_ref, q_ref, k_ref, v_ref, ckv_out_ref, kr_out_ref):
    cos = cos_ref[...]
    sin = sin_ref[...]
    cqn = _rms(cq_ref[...], gq_ref[...]).astype(BF16)
    q = _dot(cqn, wq_ref[...]) * (SCALE * LOG2E)
    ckv = _rms(ckv_ref[...], gkv_ref[...])
    ckv_out_ref[...] = ckv
    ckv_b = ckv.astype(BF16)
    kn = _dot(ckv_b, wk_ref[...])
    v_ref[...] = _dot(ckv_b, wv_ref[...]).astype(BF16)
    kr = _rope128(tail_ref[...], cos, sin)
    kr_out_ref[...] = kr[:, :MLA_ROPE]
    kr_b = kr.astype(BF16)
    for h in range(MLA_HEADS):
        q0 = h * QK_SLOT
        q_ref[:, q0:q0 + LANES] = q[:, q0:q0 + LANES].astype(BF16)
        q_ref[:, q0 + LANES:q0 + QK_SLOT] = _rope128(
            q[:, q0 + LANES:q0 + QK_SLOT], cos, sin).astype(BF16)
        k_ref[:, q0:q0 + LANES] = kn[:, h * MLA_NOPE:(h + 1) * MLA_NOPE].astype(BF16)
        k_ref[:, q0 + LANES:q0 + QK_SLOT] = kr_b


def _mla_proj(z, g_q, g_kv, wq, wk, wv, cos, sin, tm):
    t = z.shape[0]
    rows = cos.shape[0] // tm
    hq = MLA_HEADS * QK_SLOT
    qlr, kvr = wq.shape[0], wk.shape[0]
    assert qlr == kvr, "c_q and c_kv are addressed as equal-width column blocks of z"
    tail_blk = (z.shape[1] - 2 * LANES) // LANES
    const = lambda i: (0, 0)
    return pl.pallas_call(
        _mla_proj_kernel,
        out_shape=(jax.ShapeDtypeStruct((t, hq), BF16),
                   jax.ShapeDtypeStruct((t, hq), BF16),
                   jax.ShapeDtypeStruct((t, MLA_HEADS * MLA_V), BF16),
                   jax.ShapeDtypeStruct((t, kvr), F32),
                   jax.ShapeDtypeStruct((t, MLA_ROPE), F32)),
        grid=(t // tm,),
        in_specs=[pl.BlockSpec((tm, qlr), lambda i: (i, 0)),
                  pl.BlockSpec((tm, kvr), lambda i: (i, 1)),
                  pl.BlockSpec((tm, LANES), lambda i: (i, tail_blk)),
                  pl.BlockSpec((1, qlr), const),
                  pl.BlockSpec((1, kvr), const),
                  pl.BlockSpec(wq.shape, const),
                  pl.BlockSpec(wk.shape, const),
                  pl.BlockSpec(wv.shape, const),
                  pl.BlockSpec((tm, LANES), lambda i: (i % rows, 0)),
                  pl.BlockSpec((tm, LANES), lambda i: (i % rows, 0))],
        out_specs=(pl.BlockSpec((tm, hq), lambda i: (i, 0)),
                   pl.BlockSpec((tm, hq), lambda i: (i, 0)),
                   pl.BlockSpec((tm, MLA_HEADS * MLA_V), lambda i: (i, 0)),
                   pl.BlockSpec((tm, kvr), lambda i: (i, 0)),
                   pl.BlockSpec((tm, MLA_ROPE), lambda i: (i, 0))),
        compiler_params=_cparams(("arbitrary",)),
        name="mla_proj",
    )(z, z, z, g_q, g_kv, wq, wk, wv, cos, sin)


def _flash_kernel(q_ref, k_ref, v_ref, o_ref, m_ref, l_ref, acc_ref, *, heads):
    qi = pl.program_id(2)
    tq = q_ref.shape[0]
    m_ref[...] = jnp.full(m_ref.shape, -jnp.inf, F32)
    l_ref[...] = jnp.zeros(l_ref.shape, F32)
    acc_ref[...] = jnp.zeros(acc_ref.shape, F32)

    def update(j, diagonal):
        keys = pl.ds(pl.multiple_of(j * tq, tq), tq)
        for h in range(heads):
            k = k_ref[keys, h * QK_SLOT:(h + 1) * QK_SLOT]
            v = v_ref[keys, h * MLA_V:(h + 1) * MLA_V]
            s = _dot_nt(q_ref[:, h * QK_SLOT:(h + 1) * QK_SLOT], k)
            if diagonal:
                row = lax.broadcasted_iota(jnp.int32, s.shape, 0)
                col = lax.broadcasted_iota(jnp.int32, s.shape, 1)
                s = jnp.where(col <= row, s, -jnp.inf)
            m_prev = m_ref[h]
            m_new = jnp.maximum(m_prev, jnp.max(s, axis=1, keepdims=True))
            alpha = jnp.exp2(m_prev - m_new)
            p = jnp.exp2(s - m_new)
            l_ref[h] = alpha * l_ref[h] + jnp.sum(p, axis=1, keepdims=True)
            acc_ref[h] = alpha * acc_ref[h] + _dot(p.astype(BF16), v)
            m_ref[h] = m_new

    def body(j, carry):
        update(j, False)
        return carry

    lax.fori_loop(0, qi, body, 0)
    update(qi, True)
    for h in range(heads):
        o_ref[:, h * MLA_V:(h + 1) * MLA_V] = (acc_ref[h] / l_ref[h]).astype(o_ref.dtype)


def _flash(q, k, v, batch, seq, tq, heads):
    nb = seq // tq
    t = q.shape[0]
    hg = MLA_HEADS // heads
    return pl.pallas_call(
        functools.partial(_flash_kernel, heads=heads),
        out_shape=jax.ShapeDtypeStruct((t, MLA_HEADS * MLA_V), BF16),
        grid=(batch, hg, nb),
        in_specs=[pl.BlockSpec((tq, heads * QK_SLOT), lambda b, h, i: (b * nb + i, h)),
                  pl.BlockSpec((seq, heads * QK_SLOT), lambda b, h, i: (b, h)),
                  pl.BlockSpec((seq, heads * MLA_V), lambda b, h, i: (b, h))],
        out_specs=pl.BlockSpec((tq, heads * MLA_V), lambda b, h, i: (b * nb + i, h)),
        scratch_shapes=[pltpu.VMEM((heads, tq, 1), F32), pltpu.VMEM((heads, tq, 1), F32),
                        pltpu.VMEM((heads, tq, MLA_V), F32)],
        compiler_params=_cparams(("arbitrary",) * 3),
        name="flash_attn",
    )(q, k, v)


def _mlstm_chunk_kernel(q_ref, k_ref, v_ref, o_ref, gc_ref, gr_ref, bias_r_ref, bias_c_ref,
                        gml_ref, h_ref, c_out_ref, n_out_ref, m_out_ref,
                        c_sc, n_sc, m_sc, *, dk, dv):
    c_idx = pl.program_id(1)
    L = q_ref.shape[0]
    nh = ML_HEADS

    @pl.when(c_idx == 0)
    def _():
        c_sc[...] = jnp.zeros(c_sc.shape, F32)
        n_sc[...] = jnp.zeros(n_sc.shape, F32)
        m_sc[...] = jnp.zeros(m_sc.shape, F32)

    gates_c = gc_ref[...][:, :2 * nh] + bias_r_ref[...]
    gates_r = gr_ref[...] + bias_c_ref[...]
    row = lax.broadcasted_iota(jnp.int32, (L, L), 0)
    col = lax.broadcasted_iota(jnp.int32, (L, L), 1)
    lower = col <= row
    lower_f = lower.astype(F32)
    upper_f = (row <= col).astype(F32)
    hp = lax.Precision.HIGHEST
    b_col = jnp.dot(lower_f, _log_sigmoid(gates_c[:, nh:]), precision=hp,
                    preferred_element_type=F32)
    b_row = jnp.dot(_log_sigmoid(gates_r[nh:, :]), upper_f, precision=hp,
                    preferred_element_type=F32)

    for h in range(nh):
        bc = b_col[:, h:h + 1]
        br = b_row[h:h + 1, :]
        ic = gates_c[:, h:h + 1]
        ir = gates_r[h:h + 1, :]
        m_prev = m_sc[h][0:1, 0:1]
        log_d = jnp.where(lower, bc - br + ir, -jnp.inf)
        log_prev = bc + m_prev
        m_t = jnp.maximum(log_prev, jnp.max(log_d, axis=1, keepdims=True))
        q = q_ref[:, h * dk:(h + 1) * dk]
        k = k_ref[:, h * dk:(h + 1) * dk] * (dk ** -0.5)
        v = v_ref[:, h * dv:(h + 1) * dv]
        qb = q.astype(BF16)
        kb = k.astype(BF16)
        s = _dot_nt(qb, kb) * jnp.exp(log_d - m_t)
        w_prev = jnp.exp(log_prev - m_t)
        c_prev = c_sc[h]
        n_prev = n_sc[h][0:1, :]
        num = _dot(s.astype(BF16), v.astype(BF16)) + w_prev * _dot_nt(qb, c_prev.astype(BF16))
        den = jnp.sum(s, axis=1, keepdims=True) + w_prev * jnp.sum(q * n_prev, axis=1, keepdims=True)
        hh = num / jnp.maximum(jnp.abs(den), jnp.exp(-m_t))
        hh = _rms(hh, gml_ref[:, h * dv:(h + 1) * dv])
        h_ref[:, h * dv:(h + 1) * dv] = (hh * _sigmoid(o_ref[:, h * dv:(h + 1) * dv])).astype(h_ref.dtype)
        m_new = m_t[L - 1:L, :]
        b_last = bc[L - 1:L, :]
        w_s = jnp.exp(b_last - bc + ic - m_new)
        decay = jnp.exp(b_last + m_prev - m_new)
        c_sc[h] = decay * c_prev + _dot_tn((w_s * v).astype(BF16), kb)
        n_sc[h] = jnp.broadcast_to(decay * n_prev + jnp.sum(w_s * k, axis=0, keepdims=True),
                                   n_sc.shape[1:])
        m_sc[h] = jnp.broadcast_to(m_new, m_sc.shape[1:])

    @pl.when(c_idx == pl.num_programs(1) - 1)
    def _():
        c_out_ref[0] = c_sc[...]
        n_out_ref[0] = n_sc[...]
        m_out_ref[0] = m_sc[...]


def _mlstm_prompt(z, gates_row, bias_r, bias_c, g_ml, batch, seq, chunk, dk, dv):
    t = z.shape[0]
    nc = seq // chunk
    nh = ML_HEADS
    kern = functools.partial(_mlstm_chunk_kernel, dk=dk, dv=dv)
    tok = lambda w, blk: pl.BlockSpec((chunk, w), lambda b, c: (b * nc + c, blk))
    const = lambda b, c: (0, 0)
    st = lambda b, c: (b, 0, 0, 0)
    return pl.pallas_call(
        kern,
        out_shape=(jax.ShapeDtypeStruct((t, nh * dv), BF16),
                   jax.ShapeDtypeStruct((batch, nh, dv, dk), F32),
                   jax.ShapeDtypeStruct((batch, nh, 8, dk), F32),
                   jax.ShapeDtypeStruct((batch, nh, 8, LANES), F32)),
        grid=(batch, nc),
        in_specs=[tok(nh * dk, 2), tok(nh * dk, 3), tok(nh * dv, 2), tok(nh * dv, 3),
                  tok(LANES, 33),
                  pl.BlockSpec((2 * nh, chunk), lambda b, c: (0, b * nc + c)),
                  pl.BlockSpec((1, 2 * nh), const),
                  pl.BlockSpec((2 * nh, 1), const),
                  pl.BlockSpec((1, nh * dv), const)],
        out_specs=(pl.BlockSpec((chunk, nh * dv), lambda b, c: (b * nc + c, 0)),
                   pl.BlockSpec((1, nh, dv, dk), st),
                   pl.BlockSpec((1, nh, 8, dk), st),
                   pl.BlockSpec((1, nh, 8, LANES), st)),
        scratch_shapes=[pltpu.VMEM((nh, dv, dk), F32), pltpu.VMEM((nh, 8, dk), F32),
                        pltpu.VMEM((nh, 8, LANES), F32)],
        compiler_params=_cparams(("arbitrary", "arbitrary")),
        name="mlstm_chunk",
    )(z, z, z, z, z, gates_row, bias_r, bias_c, g_ml)


def _mlstm_step_kernel(q_ref, k_ref, v_ref, o_ref, g_ref, bias_ref, gml_ref, c0_ref, n0_ref,
                       m0_ref, h_ref, c_ref, n_ref, m_ref, *, dk, dv):
    nh = ML_HEADS
    rows = lax.broadcasted_iota(jnp.int32, (8, dv), 0)
    for t in range(q_ref.shape[0]):
        gates = g_ref[t][:, :2 * nh] + bias_ref[...]
        m_all = m0_ref[t]
        n_all = n0_ref[t]
        m_cols = []
        for h in range(nh):
            i_g = gates[:, h:h + 1]
            b = _log_sigmoid(gates[:, nh + h:nh + h + 1])
            m_prev = m_all[:, h:h + 1]
            q = q_ref[t][:, h * dk:(h + 1) * dk]
            k = k_ref[t][:, h * dk:(h + 1) * dk] * (dk ** -0.5)
            v = v_ref[t][:, h * dv:(h + 1) * dv]
            qb = q.astype(BF16)
            kb = k.astype(BF16)
            log_prev = b + m_prev
            m_t = jnp.maximum(log_prev, i_g)
            qk = jnp.sum(qb.astype(F32) * kb.astype(F32), axis=1, keepdims=True)
            s = qk * jnp.exp(i_g - m_t)
            w_prev = jnp.exp(log_prev - m_t)
            c_prev = c0_ref[t, h]
            n_prev = n_all[h:h + 1, :]
            q8 = jnp.broadcast_to(qb, (8, dk))
            cq = _dot_nt(q8, c_prev.astype(BF16))[0:1, :]
            num = s * v.astype(BF16).astype(F32) + w_prev * cq
            den = s + w_prev * jnp.sum(q * n_prev, axis=1, keepdims=True)
            hh = num / jnp.maximum(jnp.abs(den), jnp.exp(-m_t))
            hh = _rms(hh, gml_ref[:, h * dv:(h + 1) * dv])
            gate = _sigmoid(o_ref[t][:, h * dv:(h + 1) * dv])
            h_ref[t, :, h * dv:(h + 1) * dv] = (hh * gate).astype(h_ref.dtype)
            w_s = jnp.exp(i_g - m_t)
            decay = jnp.exp(log_prev - m_t)
            wv8 = jnp.where(rows == 0, jnp.broadcast_to(w_s * v, (8, dv)), 0.0).astype(BF16)
            k8 = jnp.broadcast_to(kb, (8, dk))
            c_ref[t, h] = decay * c_prev + _dot_tn(wv8, k8)
            n_ref[t, h:h + 1, :] = decay * n_prev + w_s * k
            m_cols.append(m_t)
        m_ref[t] = jnp.concatenate(m_cols, axis=1)


def _mlstm_sample(z3, bias_r, g_ml, c0, n0, m0, dk, dv, tb):
    nb = z3.shape[0]
    nh = ML_HEADS
    kern = functools.partial(_mlstm_step_kernel, dk=dk, dv=dv)
    tok = lambda w, blk: pl.BlockSpec((tb, 1, w), lambda b: (b, 0, blk))
    const = lambda b: (0, 0)
    return pl.pallas_call(
        kern,
        out_shape=(jax.ShapeDtypeStruct((nb, 1, nh * dv), BF16),
                   jax.ShapeDtypeStruct((nb, nh, dv, dk), F32),
                   jax.ShapeDtypeStruct((nb, nh, dk), F32),
                   jax.ShapeDtypeStruct((nb, 1, nh), F32)),
        grid=(nb // tb,),
        in_specs=[tok(nh * dk, 2), tok(nh * dk, 3), tok(nh * dv, 2), tok(nh * dv, 3),
                  tok(LANES, 33),
                  pl.BlockSpec((1, 2 * nh), const),
                  pl.BlockSpec((1, nh * dv), const),
                  pl.BlockSpec((tb, nh, dv, dk), lambda b: (b, 0, 0, 0)),
                  pl.BlockSpec((tb, nh, dk), lambda b: (b, 0, 0)),
                  pl.BlockSpec((tb, 1, nh), lambda b: (b, 0, 0))],
        out_specs=(pl.BlockSpec((tb, 1, nh * dv), lambda b: (b, 0, 0)),
                   pl.BlockSpec((tb, nh, dv, dk), lambda b: (b, 0, 0, 0)),
                   pl.BlockSpec((tb, nh, dk), lambda b: (b, 0, 0)),
                   pl.BlockSpec((tb, 1, nh), lambda b: (b, 0, 0))),
        compiler_params=_cparams(("arbitrary",)),
        name="mlstm_step",
    )(z3, z3, z3, z3, z3, bias_r, g_ml, c0, n0, m0)


def _q_absorb_kernel(q_ref, wk_ref, o_ref):
    o_ref[...] = _dot_nt(q_ref[:, :MLA_NOPE], wk_ref[...].astype(BF16))


def _q_absorb(q_cat, w_uk_flat):
    t = q_cat.shape[0]
    r = w_uk_flat.shape[0]
    return pl.pallas_call(
        _q_absorb_kernel,
        out_shape=jax.ShapeDtypeStruct((t, MLA_HEADS * r), F32),
        grid=(MLA_HEADS,),
        in_specs=[pl.BlockSpec((t, QK_SLOT), lambda h: (0, h)),
                  pl.BlockSpec((r, MLA_NOPE), lambda h: (0, h))],
        out_specs=pl.BlockSpec((t, r), lambda h: (0, h)),
        compiler_params=_cparams(("arbitrary",)),
        name="q_absorb",
    )(q_cat, w_uk_flat)


def _o_absorb_kernel(o_ref, wv_ref, out_ref):
    out_ref[...] = _dot(o_ref[...].astype(BF16), wv_ref[...].astype(BF16)).astype(out_ref.dtype)


def _o_absorb(o_lat2d, w_uv_flat):
    t = o_lat2d.shape[0]
    r = w_uv_flat.shape[0]
    return pl.pallas_call(
        _o_absorb_kernel,
        out_shape=jax.ShapeDtypeStruct((t, MLA_HEADS * MLA_V), BF16),
        grid=(MLA_HEADS,),
        in_specs=[pl.BlockSpec((t, r), lambda h: (0, h)),
                  pl.BlockSpec((r, MLA_V), lambda h: (0, h))],
        out_specs=pl.BlockSpec((t, MLA_V), lambda h: (0, h)),
        compiler_params=_cparams(("arbitrary",)),
        name="o_absorb",
    )(o_lat2d, w_uv_flat)


def _mla_decode_kernel(pt_ref, ql_ref, qr_ref, cn_ref, kn_ref, cc_hbm, ckt_hbm, o_ref,
                       cbuf, kbuf, sem, *, layer, n_seq, n_chunks, pages_per_chunk, page_size,
                       group, n_slots):
    total = (n_seq // group) * n_chunks
    pages_per_seq = n_chunks * pages_per_chunk
    kc = pages_per_chunk * page_size
    ahead = n_slots - 1

    def chunk_copies(step, slot):
        g = step // n_chunks
        c = step - g * n_chunks
        cps = []
        for a in range(group):
            base = (g * group + a) * pages_per_seq + c * pages_per_chunk
            for j in range(pages_per_chunk):
                page = pt_ref[base + j]
                dst = pl.ds(a * kc + j * page_size, page_size)
                cps.append(pltpu.make_async_copy(
                    cc_hbm.at[layer, page], cbuf.at[slot, dst], sem.at[0, slot]))
                cps.append(pltpu.make_async_copy(
                    ckt_hbm.at[layer, page], kbuf.at[slot, :, dst], sem.at[1, slot]))
        return cps

    def issue(step):
        if isinstance(step, int):
            src, slot = min(step, total - 1), step % n_slots
        else:
            src, slot = jnp.minimum(step, total - 1), lax.rem(step, n_slots)
        for cp in chunk_copies(src, slot):
            cp.start()

    for step in range(ahead):
        issue(step)

    def group_body(g, _):
        ql = [ql_ref[g * group + a] for a in range(group)]
        qr = [qr_ref[g * group + a] for a in range(group)]
        ql_b = [x.astype(BF16) for x in ql]
        qr_b = [x.astype(BF16) for x in qr]

        def chunk_body(c, carry):
            step = g * n_chunks + c
            slot = lax.rem(step, n_slots)
            issue(step + ahead)
            for cp in chunk_copies(step, slot):
                cp.wait()
            out = []
            for a in range(group):
                m, l, acc = carry[a]
                keys = pl.ds(a * kc, kc)
                cb = cbuf[slot, keys, :].astype(BF16)
                kb = kbuf[slot, :, keys].astype(BF16)
                s = _dot_nt(ql_b[a], cb) + _dot(qr_b[a], kb)
                m_new = jnp.maximum(m, jnp.max(s, axis=1, keepdims=True))
                alpha = jnp.exp2(m - m_new)
                p = jnp.exp2(s - m_new)
                l = alpha * l + jnp.sum(p, axis=1, keepdims=True)
                acc = alpha * acc + _dot(p.astype(BF16), cb)
                out.append((m_new, l, acc))
            return tuple(out)

        nh = ql[0].shape[0]
        init = tuple((jnp.full((nh, 1), -jnp.inf, F32), jnp.zeros((nh, 1), F32),
                      jnp.zeros(ql[0].shape, F32)) for _ in range(group))
        fin = lax.fori_loop(0, n_chunks, chunk_body, init)
        for a in range(group):
            m, l, acc = fin[a]
            b = g * group + a
            cn = cn_ref[b]
            kn = kn_ref[b]
            s_new = (jnp.sum(ql[a] * cn, axis=1, keepdims=True)
                     + jnp.sum(qr[a] * kn, axis=1, keepdims=True))
            m_f = jnp.maximum(m, s_new)
            alpha = jnp.exp2(m - m_f)
            p_new = jnp.exp2(s_new - m_f)
            o_ref[b] = (alpha * acc + p_new * cn) / (alpha * l + p_new)
        return 0

    lax.fori_loop(0, n_seq // group, group_body, 0)
    for step in range(total, total + ahead):
        for cp in chunk_copies(total - 1, step % n_slots):
            cp.wait()


def _mla_decode(page_table, q_lat, q_rope, ckv_new, kr_new, cache_ckv, cache_kr_t, layer,
                pages_per_chunk, group=2, n_slots=3):
    n_seq, n_pages = page_table.shape
    page_size, r = cache_ckv.shape[2], cache_ckv.shape[3]
    rope = cache_kr_t.shape[2]
    assert n_seq % group == 0 and n_pages % pages_per_chunk == 0
    n_chunks = n_pages // pages_per_chunk
    kc = pages_per_chunk * page_size
    kern = functools.partial(_mla_decode_kernel, layer=layer, n_seq=n_seq, n_chunks=n_chunks,
                             pages_per_chunk=pages_per_chunk, page_size=page_size,
                             group=group, n_slots=n_slots)
    vmem = pl.BlockSpec(memory_space=pltpu.VMEM)
    return pl.pallas_call(
        kern,
        out_shape=jax.ShapeDtypeStruct(q_lat.shape, F32),
        in_specs=[pl.BlockSpec(memory_space=pltpu.SMEM), vmem, vmem, vmem, vmem,
                  pl.BlockSpec(memory_space=pl.ANY), pl.BlockSpec(memory_space=pl.ANY)],
        out_specs=vmem,
        scratch_shapes=[pltpu.VMEM((n_slots, group * kc, r), F32),
                        pltpu.VMEM((n_slots, rope, group * kc), F32),
                        pltpu.SemaphoreType.DMA((2, n_slots))],
        compiler_params=pltpu.CompilerParams(vmem_limit_bytes=VMEM_LIMIT),
        name="mla_decode",
    )(page_table.reshape(-1), q_lat, q_rope, ckv_new, kr_new, cache_ckv, cache_kr_t)


def _out_proj_kernel(a_ref, h_ref, w1_ref, w2_ref, x_ref, g_ref, o_ref, hn_ref, *, tn):
    a = a_ref[...]
    h = h_ref[...]
    for c in range(o_ref.shape[1] // tn):
        cols = pl.ds(c * tn, tn)
        o_ref[:, cols] = x_ref[:, cols] + _dot(a, w1_ref[:, cols]) + _dot(h, w2_ref[:, cols])
    hn_ref[...] = _rms(o_ref[...], g_ref[...]).astype(hn_ref.dtype)


def _out_proj(a, h, w_out, x, g_next, tm, tn):
    t, d = x.shape
    ka = a.shape[1]
    kh = h.shape[1]
    assert ka == kh
    return pl.pallas_call(
        functools.partial(_out_proj_kernel, tn=tn),
        out_shape=(jax.ShapeDtypeStruct((t, d), F32), jax.ShapeDtypeStruct((t, d), BF16)),
        grid=(t // tm,),
        in_specs=[pl.BlockSpec((tm, ka), lambda i: (i, 0)),
                  pl.BlockSpec((tm, kh), lambda i: (i, 0)),
                  pl.BlockSpec((ka, d), lambda i: (0, 0)),
                  pl.BlockSpec((kh, d), lambda i: (1, 0)),
                  pl.BlockSpec((tm, d), lambda i: (i, 0)),
                  pl.BlockSpec((1, d), lambda i: (0, 0))],
        out_specs=(pl.BlockSpec((tm, d), lambda i: (i, 0)),
                   pl.BlockSpec((tm, d), lambda i: (i, 0))),
        compiler_params=_cparams(("arbitrary",)),
        name="out_proj",
    )(a, h, w_out, w_out, x, g_next)


def _ffn_up_kernel(*refs, seq_tiles, decode, row_split):
    if decode:
        (hn_ref, wg_ref, wu_ref, wc_ref, bc_ref, h0_ref, h1_ref,
         a_ref, u_ref, wgb_ref, wub_ref) = refs
    else:
        (hn_ref, wg_ref, wu_ref, wc_ref, bc_ref,
         a_ref, u_ref, wgb_ref, wub_ref, prev_ref) = refs
    i = pl.program_id(1)

    @pl.when(i == 0)
    def _():
        wgb_ref[...] = wg_ref[...].astype(BF16)
        wub_ref[...] = wu_ref[...].astype(BF16)

    wg = wgb_ref[...]
    wu = wub_ref[...]
    wc = wc_ref[...]
    bc = bc_ref[...]
    tm = hn_ref.shape[0]
    tail = u_ref.shape[0]
    if decode:
        u = _dot(hn_ref[...], wg)
        up = _dot(hn_ref[...], wu)
        u_ref[...] = u
        conv = h0_ref[...] * wc[0:1, :] + h1_ref[...] * wc[1:2, :] + u * wc[2:3, :] + bc
        a_ref[...] = (conv * _sigmoid(conv) * up).astype(a_ref.dtype)
        return

    @pl.when(i == 0)
    def _():
        prev_ref[...] = jnp.zeros(prev_ref.shape, F32)

    prev = jnp.where(lax.rem(i, seq_tiles) != 0, prev_ref[...], 0.0)
    sub = tm // row_split
    for r in range(row_split):
        rows = pl.ds(r * sub, sub)
        hn = hn_ref[rows, :]
        u = _dot(hn, wg)
        up = _dot(hn, wu)
        row = lax.broadcasted_iota(jnp.int32, u.shape, 0)
        p6 = jnp.broadcast_to(prev[6:7, :], u.shape)
        p7 = jnp.broadcast_to(prev[7:8, :], u.shape)
        u_m1 = jnp.where(row == 0, p7, pltpu.roll(u, 1, axis=0))
        u_m2 = jnp.where(row == 0, p6, jnp.where(row == 1, p7, pltpu.roll(u, 2, axis=0)))
        conv = u_m2 * wc[0:1, :] + u_m1 * wc[1:2, :] + u * wc[2:3, :] + bc
        a_ref[rows, :] = (conv * _sigmoid(conv) * up).astype(a_ref.dtype)
        prev = u[sub - 8:, :]
    prev_ref[...] = prev
    u_ref[...] = prev[8 - tail:, :]


def _ffn_up(hn, w_gate, w_up, w_conv, b_conv, tm, tf, seq=None, hist=None):
    t, d = hn.shape
    ff = w_gate.shape[1]
    decode = hist is not None
    tail = tm if decode else 8
    kern = functools.partial(_ffn_up_kernel, seq_tiles=None if decode else seq // tm,
                             decode=decode, row_split=4)
    in_specs = [pl.BlockSpec((tm, d), lambda f, i: (i, 0)),
                pl.BlockSpec((d, tf), lambda f, i: (0, f)),
                pl.BlockSpec((d, tf), lambda f, i: (0, f)),
                pl.BlockSpec((CONV_W, tf), lambda f, i: (0, f)),
                pl.BlockSpec((1, tf), lambda f, i: (0, f))]
    args = [hn, w_gate, w_up, w_conv, b_conv]
    scratch = [pltpu.VMEM((d, tf), BF16), pltpu.VMEM((d, tf), BF16)]
    if decode:
        in_specs += [pl.BlockSpec((tm, tf), lambda f, i: (i, f))] * 2
        args += list(hist)
    else:
        scratch.append(pltpu.VMEM((8, tf), F32))
    return pl.pallas_call(
        kern,
        out_shape=(jax.ShapeDtypeStruct((t, ff), BF16),
                   jax.ShapeDtypeStruct((t // tm * tail, ff), F32)),
        grid=(ff // tf, t // tm),
        in_specs=in_specs,
        out_specs=(pl.BlockSpec((tm, tf), lambda f, i: (i, f)),
                   pl.BlockSpec((tail, tf), lambda f, i: (i, f))),
        scratch_shapes=scratch,
        compiler_params=_cparams(("arbitrary", "arbitrary")),
        name="ffn_up",
    )(*args)


def _ffn_down_kernel(a_ref, w_ref, x_ref, g_ref, y_ref, *, tn):
    k = pl.program_id(1)

    @pl.when(k == 0)
    def _():
        y_ref[...] = x_ref[...]

    a = a_ref[...]
    for c in range(y_ref.shape[1] // tn):
        cols = pl.ds(c * tn, tn)
        y_ref[:, cols] += _dot(a, w_ref[:, cols].astype(BF16))

    @pl.when(k == pl.num_programs(1) - 1)
    def _():
        y_ref[...] = _rms(y_ref[...], g_ref[...])


def _ffn_down(a, w_down, x, g_final, tm, tk, tn):
    t, d = x.shape
    ff = a.shape[1]
    return pl.pallas_call(
        functools.partial(_ffn_down_kernel, tn=tn),
        out_shape=jax.ShapeDtypeStruct((t, d), F32),
        grid=(t // tm, ff // tk),
        in_specs=[pl.BlockSpec((tm, tk), lambda i, k: (i, k)),
                  pl.BlockSpec((tk, d), lambda i, k: (k, 0)),
                  pl.BlockSpec((tm, d), lambda i, k: (i, 0)),
                  pl.BlockSpec((1, d), lambda i, k: (0, 0))],
        out_specs=pl.BlockSpec((tm, d), lambda i, k: (i, 0)),
        compiler_params=_cparams(("arbitrary", "arbitrary")),
        name="ffn_down",
    )(a, w_down, x, g_final)


def _rope_tables(pos):
    inv = ROPE_THETA ** (-jnp.arange(0, MLA_ROPE, 2, dtype=F32) / MLA_ROPE)
    ang = pos.astype(F32)[:, None] * inv[None, :]
    cos, sin = jnp.cos(ang), jnp.sin(ang)
    zero = jnp.zeros_like(cos)
    return (jnp.concatenate([cos, cos, zero, zero], axis=1),
            jnp.concatenate([-sin, sin, zero, zero], axis=1))


def _prep_w_in(w_in, q_lora, kv_rank, dk, dv):
    nh = ML_HEADS
    o_kr = q_lora + kv_rank
    o_qm = o_kr + MLA_ROPE
    o_gate = o_qm + 2 * nh * dk + 2 * nh * dv
    half = MLA_ROPE // 2
    w_t = w_in.T.astype(BF16)
    pad = jnp.zeros((LANES - 2 * nh, w_in.shape[0]), BF16)
    rows = [w_t[:o_kr], w_t[o_qm:o_gate],
            w_t[o_kr:o_qm], w_t[o_kr + half:o_qm], w_t[o_kr:o_kr + half],
            w_t[o_gate:], pad]
    return jnp.concatenate(rows, axis=0)


def _prep_w_uq(w_uq):
    hd = MLA_NOPE + MLA_ROPE
    half = MLA_ROPE // 2
    cols = []
    for h in range(MLA_HEADS):
        o = h * hd
        cols += [w_uq[:, o:o + hd], w_uq[:, o + MLA_NOPE + half:o + hd],
                 w_uq[:, o + MLA_NOPE:o + MLA_NOPE + half]]
    return jnp.concatenate(cols, axis=1).astype(BF16)


def _layer(x, pos, lw, prep, mode, state):
    (g_mix, w_in, g_q, w_uq, g_kv, w_uk, w_uv, b_i, b_f, g_ml, w_out,
     g_ffn, w_gate, w_up, w_conv, b_conv, w_down) = lw
    w_in_p, w_uq_p, w_uk_f, w_uv_f, w_out_b = prep
    t, d = x.shape
    nh = ML_HEADS
    dk = (w_in.shape[1] - g_q.shape[0] - g_kv.shape[0] - MLA_ROPE - 2 * nh) // (6 * nh)
    dv = 2 * dk
    prompt = mode == "prompt"
    tm = 512 if prompt else t
    cos, sin = _rope_tables(pos)
    if not prompt:
        cos = jnp.broadcast_to(cos, (t, LANES))
        sin = jnp.broadcast_to(sin, (t, LANES))

    z = _proj_in(x, g_mix[None, :], w_in_p, tm, w_in_p.shape[0] // 2)
    q_cat, k_cat, v, ckv, kr = _mla_proj(
        z, g_q[None, :], g_kv[None, :], w_uq_p, w_uk_f.astype(BF16), w_uv_f.astype(BF16),
        cos, sin, tm)
    bias_r = jnp.concatenate([b_i, b_f])[None, :].astype(F32)

    if prompt:
        batch, seq = state["batch"], state["seq"]
        o_mla = _flash(q_cat, k_cat, v, batch, seq, min(seq, 1024), 2)
        gates_row = z[:, 33 * LANES:33 * LANES + 2 * nh].T
        h_ml, c_new, n_new, m_new = _mlstm_prompt(
            z, gates_row, bias_r, bias_r.T, g_ml[None, :], batch, seq, math.gcd(seq, 256), dk, dv)
        n_new = n_new[:, :, 0, :]
        m_new = m_new[:, :, 0, 0]
    else:
        r = g_kv.shape[0]
        q_lat = _q_absorb(q_cat, w_uk_f).reshape(t, MLA_HEADS, r)
        q_rope = q_cat.reshape(t, MLA_HEADS, QK_SLOT)[:, :, MLA_NOPE:MLA_NOPE + MLA_ROPE].astype(F32)
        o_lat = _mla_decode(state["page_table"], q_lat, q_rope, ckv[:, None, :], kr[:, None, :],
                            state["cache_ckv"], state["cache_kr_t"], state["layer"], 16)
        o_mla = _o_absorb(o_lat.reshape(t, MLA_HEADS * r), w_uv_f)
        h_ml, c_new, n_new, m_new = _mlstm_sample(
            z[:, None, :], bias_r, g_ml[None, :], state["C"], state["n"], state["m"][:, None, :],
            dk, dv, 1)
        h_ml = h_ml[:, 0, :]
        m_new = m_new[:, 0, :]

    x1, hn = _out_proj(o_mla, h_ml, w_out_b, x, g_ffn[None, :], min(t, 512), 512)
    if prompt:
        tmf = 1024
        a, u_tail = _ffn_up(hn, w_gate, w_up, w_conv, b_conv[None, :], tmf, 512, seq=seq)
        tiles = seq // tmf
        u_tail = u_tail.reshape(batch, tiles, 8, -1)
        conv_new = u_tail[:, tiles - 1, 8 - (CONV_W - 1):, :]
    else:
        conv0 = state["conv"]
        a, u = _ffn_up(hn, w_gate, w_up, w_conv, b_conv[None, :], t, 512,
                       hist=(conv0[:, 0, :], conv0[:, 1, :]))
        conv_new = jnp.stack([conv0[:, 1, :], u], axis=1)
    y = _ffn_down(a, w_down, x1, state["g_out"][None, :], min(t, 1024), 512, 512)
    return y, (ckv, kr, c_new, n_new, m_new, conv_new)


def kernel(x_prompt, x_sample, cache_ckv, cache_kr, state_C, state_n, state_m, state_conv, page_table, g_mix, w_in, g_q, w_uq, g_kv, w_uk, w_uv, b_i, b_f, g_ml, w_out, g_ffn, w_gate, w_up, w_conv, b_conv, w_down, g_final):
    depth = w_in.shape[0]
    assert depth == 1, "the final RMSNorm is fused into the (single) layer's FFN kernel"
    bp, sp, d = x_prompt.shape
    bs, ss, _ = x_sample.shape
    assert ss == 1, "the sample group decodes one token per sequence"
    past_len = page_table.shape[1] * cache_ckv.shape[2]
    pos_p = jnp.arange(sp)
    pos_s = past_len + jnp.arange(ss)
    l = 0
    lw = (g_mix[l], w_in[l], g_q[l], w_uq[l], g_kv[l], w_uk[l], w_uv[l], b_i[l], b_f[l],
          g_ml[l], w_out[l], g_ffn[l], w_gate[l], w_up[l], w_conv[l], b_conv[l], w_down[l])
    q_lora, kv_rank = g_q.shape[1], g_kv.shape[1]
    nh = ML_HEADS
    dk = (w_in.shape[2] - q_lora - kv_rank - MLA_ROPE - 2 * nh) // (6 * nh)
    prep = (_prep_w_in(w_in[l], q_lora, kv_rank, dk, 2 * dk), _prep_w_uq(w_uq[l]),
            w_uk[l].reshape(kv_rank, -1), w_uv[l].reshape(kv_rank, -1), w_out[l].astype(BF16))
    cache_kr_t = jnp.swapaxes(cache_kr, 2, 3)

    yp, new_p = _layer(x_prompt.reshape(bp * sp, d), pos_p, lw, prep, "prompt",
                       dict(batch=bp, seq=sp, g_out=g_final))
    ys, new_s = _layer(x_sample.reshape(bs * ss, d), pos_s, lw, prep, "sample",
                       dict(page_table=page_table, cache_ckv=cache_ckv, cache_kr_t=cache_kr_t,
                            layer=l, C=state_C[l], n=state_n[l], m=state_m[l],
                            conv=state_conv[l], g_out=g_final))
    ckv_p, kr_p, c_p, n_p, m_p, conv_p = new_p
    ckv_s, kr_s, c_s, n_s, m_s, conv_s = new_s
    return (yp.reshape(bp, sp, d), ys.reshape(bs, ss, d),
            ckv_p.reshape(1, bp, sp, -1), kr_p.reshape(1, bp, sp, -1),
            c_p[None], n_p[None], m_p[None], conv_p[None],
            ckv_s.reshape(1, bs, ss, -1), kr_s.reshape(1, bs, ss, -1),
            c_s[None], n_s[None], m_s[None], conv_s[None])
```

```python
import functools
import math

import jax
import jax.numpy as jnp
from jax import lax
from jax.experimental import pallas as pl
from jax.experimental.pallas import tpu as pltpu

F32 = jnp.float32
BF16 = jnp.bfloat16

MLA_HEADS = 8
MLA_NOPE = 128
MLA_ROPE = 64
MLA_V = 128
ML_HEADS = 4
CONV_W = 3
ROPE_THETA = 10000.0
EPS = 1e-6
SCALE = (MLA_NOPE + MLA_ROPE) ** -0.5
LOG2E = math.log2(math.e)

LANES = 128
QK_SLOT = 2 * LANES
VMEM_LIMIT = 56 * 1024 * 1024


def _cparams(sem, vmem=VMEM_LIMIT):
    return pltpu.CompilerParams(dimension_semantics=sem, vmem_limit_bytes=vmem)


def _rms(x, g):
    return x * lax.rsqrt(jnp.mean(x * x, axis=-1, keepdims=True) + EPS) * g


def _sigmoid(x):
    return 1.0 / (1.0 + jnp.exp(-x))


def _log_sigmoid(x):
    return jnp.minimum(x, 0.0) - jnp.log1p(jnp.exp(-jnp.abs(x)))


def _dot(a, b):
    return jnp.dot(a, b, preferred_element_type=F32)


def _dot_nt(a, b):
    return lax.dot_general(a, b, (((1,), (1,)), ((), ())), preferred_element_type=F32)


def _dot_tn(a, b):
    return lax.dot_general(a, b, (((0,), (0,)), ((), ())), preferred_element_type=F32)


def _proj_in_kernel(x_ref, g_ref, w_ref, z_ref, xn_ref):
    @pl.when(pl.program_id(1) == 0)
    def _():
        xn_ref[...] = _rms(x_ref[...], g_ref[...]).astype(BF16)

    z_ref[...] = _dot_nt(xn_ref[...], w_ref[...])


def _proj_in(x, g, w_t, tm, tn):
    t, d = x.shape
    n = w_t.shape[0]
    return pl.pallas_call(
        _proj_in_kernel,
        out_shape=jax.ShapeDtypeStruct((t, n), F32),
        grid=(t // tm, n // tn),
        in_specs=[pl.BlockSpec((tm, d), lambda i, j: (i, 0)),
                  pl.BlockSpec((1, d), lambda i, j: (0, 0)),
                  pl.BlockSpec((tn, d), lambda i, j: (j, 0))],
        out_specs=pl.BlockSpec((tm, tn), lambda i, j: (i, j)),
        scratch_shapes=[pltpu.VMEM((tm, d), BF16)],
        compiler_params=_cparams(("arbitrary", "arbitrary")),
        name="proj_in",
    )(x, g, w_t)


def _rope128(x, cos, sin):
    return x * cos + pltpu.roll(x, 64, axis=1) * sin


def _mla_proj_kernel(cq_ref, ckv_ref, tail_ref, gq_ref, gkv_ref, wq_ref, wk_ref, wv_ref,
                     cos_ref, sin_ref, q_ref, k_ref, v_ref, ckv_out_ref, kr_out_ref):
    cos = cos_ref[...]
    sin = sin_ref[...]
    cqn = _rms(cq_ref[...], gq_ref[...]).astype(BF16)
    q = _dot(cqn, wq_ref[...]) * (SCALE * LOG2E)
    ckv = _rms(ckv_ref[...], gkv_ref[...])
    ckv_out_ref[...] = ckv
    ckv_b = ckv.astype(BF16)
    kn = _dot(ckv_b, wk_ref[...])
    v_ref[...] = _dot_nt(wv_ref[...], ckv_b).astype(BF16)
    kr = _rope128(tail_ref[...], cos, sin)
    kr_out_ref[...] = kr[:, :MLA_ROPE]
    kr_b = kr.astype(BF16)
    for h in range(MLA_HEADS):
        q0 = h * QK_SLOT
        q_ref[:, q0:q0 + LANES] = q[:, q0:q0 + LANES].astype(BF16)
        q_ref[:, q0 + LANES:q0 + QK_SLOT] = _rope128(
            q[:, q0 + LANES:q0 + QK_SLOT], cos, sin).astype(BF16)
        k_ref[:, q0:q0 + LANES] = kn[:, h * MLA_NOPE:(h + 1) * MLA_NOPE].astype(BF16)
        k_ref[:, q0 + LANES:q0 + QK_SLOT] = kr_b


def _mla_proj(z, g_q, g_kv, wq, wk, wv_t, cos, sin, tm):
    wv = wv_t
    t = z.shape[0]
    rows = cos.shape[0] // tm
    hq = MLA_HEADS * QK_SLOT
    qlr, kvr = wq.shape[0], wk.shape[0]
    assert qlr == kvr, "c_q and c_kv are addressed as equal-width column blocks of z"
    tail_blk = (z.shape[1] - 2 * LANES) // LANES
    const = lambda i: (0, 0)
    return pl.pallas_call(
        _mla_proj_kernel,
        out_shape=(jax.ShapeDtypeStruct((t, hq), BF16),
                   jax.ShapeDtypeStruct((t, hq), BF16),
                   jax.ShapeDtypeStruct((MLA_HEADS * MLA_V, t), BF16),
                   jax.ShapeDtypeStruct((t, kvr), F32),
                   jax.ShapeDtypeStruct((t, MLA_ROPE), F32)),
        grid=(t // tm,),
        in_specs=[pl.BlockSpec((tm, qlr), lambda i: (i, 0)),
                  pl.BlockSpec((tm, kvr), lambda i: (i, 1)),
                  pl.BlockSpec((tm, LANES), lambda i: (i, tail_blk)),
                  pl.BlockSpec((1, qlr), const),
                  pl.BlockSpec((1, kvr), const),
                  pl.BlockSpec(wq.shape, const),
                  pl.BlockSpec(wk.shape, const),
                  pl.BlockSpec(wv.shape, const),
                  pl.BlockSpec((tm, LANES), lambda i: (i % rows, 0)),
                  pl.BlockSpec((tm, LANES), lambda i: (i % rows, 0))],
        out_specs=(pl.BlockSpec((tm, hq), lambda i: (i, 0)),
                   pl.BlockSpec((tm, hq), lambda i: (i, 0)),
                   pl.BlockSpec((MLA_HEADS * MLA_V, tm), lambda i: (0, i)),
                   pl.BlockSpec((tm, kvr), lambda i: (i, 0)),
                   pl.BlockSpec((tm, MLA_ROPE), lambda i: (i, 0))),
        compiler_params=_cparams(("arbitrary",)),
        name="mla_proj",
    )(z, z, z, g_q, g_kv, wq, wk, wv, cos, sin)


def _flash_kernel(q_ref, k_ref, vt_ref, o_ref, m_ref, l_ref, acc_ref, *, heads):
    qi = pl.program_id(2)
    tq = q_ref.shape[0]
    m_ref[...] = jnp.full(m_ref.shape, -jnp.inf, F32)
    l_ref[...] = jnp.zeros(l_ref.shape, F32)
    acc_ref[...] = jnp.zeros(acc_ref.shape, F32)

    def update(j, diagonal):
        keys = pl.ds(pl.multiple_of(j * tq, tq), tq)
        for h in range(heads):
            k = k_ref[keys, h * QK_SLOT:(h + 1) * QK_SLOT]
            vt = vt_ref[h * MLA_V:(h + 1) * MLA_V, keys]
            st = _dot_nt(k, q_ref[:, h * QK_SLOT:(h + 1) * QK_SLOT])
            if diagonal:
                key = lax.broadcasted_iota(jnp.int32, st.shape, 0)
                qry = lax.broadcasted_iota(jnp.int32, st.shape, 1)
                st = jnp.where(key <= qry, st, -jnp.inf)
            m_prev = m_ref[h]
            m_new = jnp.maximum(m_prev, jnp.max(st, axis=0, keepdims=True))
            alpha = jnp.exp2(m_prev - m_new)
            pt = jnp.exp2(st - m_new)
            l_ref[h] = alpha * l_ref[h] + jnp.sum(pt, axis=0, keepdims=True)
            acc_ref[h] = alpha * acc_ref[h] + _dot(vt, pt.astype(BF16))
            m_ref[h] = m_new

    def body(j, carry):
        update(j, False)
        return carry

    lax.fori_loop(0, qi, body, 0)
    update(qi, True)
    for h in range(heads):
        o_ref[:, h * MLA_V:(h + 1) * MLA_V] = (acc_ref[h] / l_ref[h]).T.astype(o_ref.dtype)


def _flash(q, k, v_t, batch, seq, tq, heads):
    nb = seq // tq
    t = q.shape[0]
    hg = MLA_HEADS // heads
    return pl.pallas_call(
        functools.partial(_flash_kernel, heads=heads),
        out_shape=jax.ShapeDtypeStruct((t, MLA_HEADS * MLA_V), BF16),
        grid=(batch, hg, nb),
        in_specs=[pl.BlockSpec((tq, heads * QK_SLOT), lambda b, h, i: (b * nb + i, h)),
                  pl.BlockSpec((seq, heads * QK_SLOT), lambda b, h, i: (b, h)),
                  pl.BlockSpec((heads * MLA_V, seq), lambda b, h, i: (h, b))],
        out_specs=pl.BlockSpec((tq, heads * MLA_V), lambda b, h, i: (b * nb + i, h)),
        scratch_shapes=[pltpu.VMEM((heads, 1, tq), F32), pltpu.VMEM((heads, 1, tq), F32),
                        pltpu.VMEM((heads, MLA_V, tq), F32)],
        compiler_params=_cparams(("arbitrary",) * 3),
        name="flash_attn",
    )(q, k, v_t)


def _mlstm_chunk_kernel(q_ref, k_ref, v_ref, o_ref, gc_ref, gr_ref, bias_r_ref, bias_c_ref,
                        gml_ref, h_ref, c_out_ref, n_out_ref, m_out_ref,
                        c_sc, n_sc, m_sc, *, dk, dv):
    c_idx = pl.program_id(1)
    L = q_ref.shape[0]
    nh = ML_HEADS

    @pl.when(c_idx == 0)
    def _():
        c_sc[...] = jnp.zeros(c_sc.shape, F32)
        n_sc[...] = jnp.zeros(n_sc.shape, F32)
        m_sc[...] = jnp.zeros(m_sc.shape, F32)

    gates_c = gc_ref[...][:, :2 * nh] + bias_r_ref[...]
    gates_r = gr_ref[...] + bias_c_ref[...]
    row = lax.broadcasted_iota(jnp.int32, (L, L), 0)
    col = lax.broadcasted_iota(jnp.int32, (L, L), 1)
    lower = col <= row
    lower_f = lower.astype(F32)
    upper_f = (row <= col).astype(F32)
    hp = lax.Precision.HIGHEST
    b_col = jnp.dot(lower_f, _log_sigmoid(gates_c[:, nh:]), precision=hp,
                    preferred_element_type=F32)
    b_row = jnp.dot(_log_sigmoid(gates_r[nh:, :]), upper_f, precision=hp,
                    preferred_element_type=F32)

    for h in range(nh):
        bc = b_col[:, h:h + 1]
        br = b_row[h:h + 1, :]
        ic = gates_c[:, h:h + 1]
        ir = gates_r[h:h + 1, :]
        m_prev = m_sc[h][0:1, 0:1]
        log_d = jnp.where(lower, bc - br + ir, -jnp.inf)
        log_prev = bc + m_prev
        m_t = jnp.maximum(log_prev, jnp.max(log_d, axis=1, keepdims=True))
        q = q_ref[:, h * dk:(h + 1) * dk]
        k = k_ref[:, h * dk:(h + 1) * dk] * (dk ** -0.5)
        v = v_ref[:, h * dv:(h + 1) * dv]
        qb = q.astype(BF16)
        kb = k.astype(BF16)
        s = _dot_nt(qb, kb) * jnp.exp(log_d - m_t)
        w_prev = jnp.exp(log_prev - m_t)
        c_prev = c_sc[h]
        n_prev = n_sc[h][0:1, :]
        num = _dot(s.astype(BF16), v.astype(BF16)) + w_prev * _dot_nt(qb, c_prev.astype(BF16))
        den = jnp.sum(s, axis=1, keepdims=True) + w_prev * jnp.sum(q * n_prev, axis=1, keepdims=True)
        hh = num / jnp.maximum(jnp.abs(den), jnp.exp(-m_t))
        hh = _rms(hh, gml_ref[:, h * dv:(h + 1) * dv])
        h_ref[:, h * dv:(h + 1) * dv] = (hh * _sigmoid(o_ref[:, h * dv:(h + 1) * dv])).astype(h_ref.dtype)
        m_new = m_t[L - 1:L, :]
        b_last = bc[L - 1:L, :]
        w_s = jnp.exp(b_last - bc + ic - m_new)
        decay = jnp.exp(b_last + m_prev - m_new)
        c_sc[h] = decay * c_prev + _dot_tn((w_s * v).astype(BF16), kb)
        n_sc[h] = jnp.broadcast_to(decay * n_prev + jnp.sum(w_s * k, axis=0, keepdims=True),
                                   n_sc.shape[1:])
        m_sc[h] = jnp.broadcast_to(m_new, m_sc.shape[1:])

    @pl.when(c_idx == pl.num_programs(1) - 1)
    def _():
        c_out_ref[0] = c_sc[...]
        n_out_ref[0] = n_sc[...]
        m_out_ref[0] = m_sc[...]


def _mlstm_prompt(z, gates_row, bias_r, bias_c, g_ml, batch, seq, chunk, dk, dv):
    t = z.shape[0]
    nc = seq // chunk
    nh = ML_HEADS
    kern = functools.partial(_mlstm_chunk_kernel, dk=dk, dv=dv)
    tok = lambda w, blk: pl.BlockSpec((chunk, w), lambda b, c: (b * nc + c, blk))
    const = lambda b, c: (0, 0)
    st = lambda b, c: (b, 0, 0, 0)
    return pl.pallas_call(
        kern,
        out_shape=(jax.ShapeDtypeStruct((t, nh * dv), BF16),
                   jax.ShapeDtypeStruct((batch, nh, dv, dk), F32),
                   jax.ShapeDtypeStruct((batch, nh, 8, dk), F32),
                   jax.ShapeDtypeStruct((batch, nh, 8, LANES), F32)),
        grid=(batch, nc),
        in_specs=[tok(nh * dk, 2), tok(nh * dk, 3), tok(nh * dv, 2), tok(nh * dv, 3),
                  tok(LANES, 33),
                  pl.BlockSpec((2 * nh, chunk), lambda b, c: (0, b * nc + c)),
                  pl.BlockSpec((1, 2 * nh), const),
                  pl.BlockSpec((2 * nh, 1), const),
                  pl.BlockSpec((1, nh * dv), const)],
        out_specs=(pl.BlockSpec((chunk, nh * dv), lambda b, c: (b * nc + c, 0)),
                   pl.BlockSpec((1, nh, dv, dk), st),
                   pl.BlockSpec((1, nh, 8, dk), st),
                   pl.BlockSpec((1, nh, 8, LANES), st)),
        scratch_shapes=[pltpu.VMEM((nh, dv, dk), F32), pltpu.VMEM((nh, 8, dk), F32),
                        pltpu.VMEM((nh, 8, LANES), F32)],
        compiler_params=_cparams(("arbitrary", "arbitrary")),
        name="mlstm_chunk",
    )(z, z, z, z, z, gates_row, bias_r, bias_c, g_ml)


def _mlstm_step_kernel(q_ref, k_ref, v_ref, o_ref, g_ref, bias_ref, gml_ref, c0_ref, n0_ref,
                       m0_ref, h_ref, c_ref, n_ref, m_ref, *, dk, dv):
    nh = ML_HEADS
    bd_lo = lax.broadcasted_iota(jnp.int32, (8, nh * dk), 0) * dk
    bd_col = lax.broadcasted_iota(jnp.int32, (8, nh * dk), 1)
    bd_mask = (bd_col >= bd_lo) & (bd_col < bd_lo + dk)
    v_row = lax.broadcasted_iota(jnp.int32, (8, dv), 0)
    for t in range(q_ref.shape[0]):
        gates = g_ref[t][:, :2 * nh] + bias_ref[...]
        m_all = m0_ref[t]
        n_all = n0_ref[t]
        q_all = q_ref[t]
        k_all = k_ref[t] * (dk ** -0.5)
        q_bd = jnp.where(bd_mask, jnp.broadcast_to(q_all, bd_mask.shape), 0.0).astype(BF16)
        k_bd = jnp.where(bd_mask, jnp.broadcast_to(k_all, bd_mask.shape), 0.0).astype(BF16)
        c_cat = jnp.concatenate([c0_ref[t, h] for h in range(nh)], axis=1)
        cq_all = _dot_nt(q_bd, c_cat.astype(BF16))
        m_cols = []
        wv_rows = jnp.zeros((8, dv), F32)
        decays = []
        for h in range(nh):
            i_g = gates[:, h:h + 1]
            b = _log_sigmoid(gates[:, nh + h:nh + h + 1])
            m_prev = m_all[:, h:h + 1]
            q = q_all[:, h * dk:(h + 1) * dk]
            k = k_all[:, h * dk:(h + 1) * dk]
            v = v_ref[t][:, h * dv:(h + 1) * dv]
            log_prev = b + m_prev
            m_t = jnp.maximum(log_prev, i_g)
            qk = jnp.sum(q.astype(BF16).astype(F32) * k.astype(BF16).astype(F32),
                         axis=1, keepdims=True)
            s = qk * jnp.exp(i_g - m_t)
            w_prev = jnp.exp(log_prev - m_t)
            n_prev = n_all[h:h + 1, :]
            num = s * v.astype(BF16).astype(F32) + w_prev * cq_all[h:h + 1, :]
            den = s + w_prev * jnp.sum(q * n_prev, axis=1, keepdims=True)
            hh = num / jnp.maximum(jnp.abs(den), jnp.exp(-m_t))
            hh = _rms(hh, gml_ref[:, h * dv:(h + 1) * dv])
            gate = _sigmoid(o_ref[t][:, h * dv:(h + 1) * dv])
            h_ref[t, :, h * dv:(h + 1) * dv] = (hh * gate).astype(h_ref.dtype)
            w_s = jnp.exp(i_g - m_t)
            decay = jnp.exp(log_prev - m_t)
            wv_rows = jnp.where(v_row == h, jnp.broadcast_to(w_s * v, (8, dv)), wv_rows)
            n_ref[t, h:h + 1, :] = decay * n_prev + w_s * k
            decays.append(decay)
            m_cols.append(m_t)
        outer = _dot_tn(wv_rows.astype(BF16), k_bd)
        for h in range(nh):
            c_ref[t, h] = decays[h] * c0_ref[t, h] + outer[:, h * dk:(h + 1) * dk]
        m_ref[t] = jnp.concatenate(m_cols, axis=1)


def _mlstm_sample(z3, bias_r, g_ml, c0, n0, m0, dk, dv, tb):
    nb = z3.shape[0]
    nh = ML_HEADS
    kern = functools.partial(_mlstm_step_kernel, dk=dk, dv=dv)
    tok = lambda w, blk: pl.BlockSpec((tb, 1, w), lambda b: (b, 0, blk))
    const = lambda b: (0, 0)
    return pl.pallas_call(
        kern,
        out_shape=(jax.ShapeDtypeStruct((nb, 1, nh * dv), BF16),
                   jax.ShapeDtypeStruct((nb, nh, dv, dk), F32),
                   jax.ShapeDtypeStruct((nb, nh, dk), F32),
                   jax.ShapeDtypeStruct((nb, 1, nh), F32)),
        grid=(nb // tb,),
        in_specs=[tok(nh * dk, 2), tok(nh * dk, 3), tok(nh * dv, 2), tok(nh * dv, 3),
                  tok(LANES, 33),
                  pl.BlockSpec((1, 2 * nh), const),
                  pl.BlockSpec((1, nh * dv), const),
                  pl.BlockSpec((tb, nh, dv, dk), lambda b: (b, 0, 0, 0)),
                  pl.BlockSpec((tb, nh, dk), lambda b: (b, 0, 0)),
                  pl.BlockSpec((tb, 1, nh), lambda b: (b, 0, 0))],
        out_specs=(pl.BlockSpec((tb, 1, nh * dv), lambda b: (b, 0, 0)),
                   pl.BlockSpec((tb, nh, dv, dk), lambda b: (b, 0, 0, 0)),
                   pl.BlockSpec((tb, nh, dk), lambda b: (b, 0, 0)),
                   pl.BlockSpec((tb, 1, nh), lambda b: (b, 0, 0))),
        compiler_params=_cparams(("arbitrary",)),
        name="mlstm_step",
    )(z3, z3, z3, z3, z3, bias_r, g_ml, c0, n0, m0)


def _q_absorb_kernel(q_ref, wk_ref, o_ref):
    o_ref[...] = _dot_nt(q_ref[:, :MLA_NOPE], wk_ref[...].astype(BF16))


def _q_absorb(q_cat, w_uk_flat):
    t = q_cat.shape[0]
    r = w_uk_flat.shape[0]
    return pl.pallas_call(
        _q_absorb_kernel,
        out_shape=jax.ShapeDtypeStruct((t, MLA_HEADS * r), F32),
        grid=(MLA_HEADS,),
        in_specs=[pl.BlockSpec((t, QK_SLOT), lambda h: (0, h)),
                  pl.BlockSpec((r, MLA_NOPE), lambda h: (0, h))],
        out_specs=pl.BlockSpec((t, r), lambda h: (0, h)),
        compiler_params=_cparams(("arbitrary",)),
        name="q_absorb",
    )(q_cat, w_uk_flat)


def _o_absorb_kernel(o_ref, wv_ref, out_ref):
    out_ref[...] = _dot(o_ref[...].astype(BF16), wv_ref[...].astype(BF16)).astype(out_ref.dtype)


def _o_absorb(o_lat2d, w_uv_flat):
    t = o_lat2d.shape[0]
    r = w_uv_flat.shape[0]
    return pl.pallas_call(
        _o_absorb_kernel,
        out_shape=jax.ShapeDtypeStruct((t, MLA_HEADS * MLA_V), BF16),
        grid=(MLA_HEADS,),
        in_specs=[pl.BlockSpec((t, r), lambda h: (0, h)),
                  pl.BlockSpec((r, MLA_V), lambda h: (0, h))],
        out_specs=pl.BlockSpec((t, MLA_V), lambda h: (0, h)),
        compiler_params=_cparams(("arbitrary",)),
        name="o_absorb",
    )(o_lat2d, w_uv_flat)


def _mla_decode_kernel(pt_ref, ql_ref, qr_ref, cn_ref, kn_ref, cc_hbm, ckt_hbm, o_ref,
                       cbuf, kbuf, sem, *, layer, n_seq, n_chunks, pages_per_chunk, page_size,
                       group, n_slots):
    total = (n_seq // group) * n_chunks
    pages_per_seq = n_chunks * pages_per_chunk
    kc = pages_per_chunk * page_size
    ahead = n_slots - 1

    def chunk_copies(step, slot):
        g = step // n_chunks
        c = step - g * n_chunks
        cps = []
        for a in range(group):
            base = (g * group + a) * pages_per_seq + c * pages_per_chunk
            for j in range(pages_per_chunk):
                page = pt_ref[base + j]
                dst = pl.ds(a * kc + j * page_size, page_size)
                cps.append(pltpu.make_async_copy(
                    cc_hbm.at[layer, page], cbuf.at[slot, dst], sem.at[0, slot]))
                cps.append(pltpu.make_async_copy(
                    ckt_hbm.at[layer, page], kbuf.at[slot, :, dst], sem.at[1, slot]))
        return cps

    def issue(step):
        if isinstance(step, int):
            src, slot = min(step, total - 1), step % n_slots
        else:
            src, slot = jnp.minimum(step, total - 1), lax.rem(step, n_slots)
        for cp in chunk_copies(src, slot):
            cp.start()

    for step in range(ahead):
        issue(step)

    def group_body(g, _):
        ql = [ql_ref[g * group + a] for a in range(group)]
        qr = [qr_ref[g * group + a] for a in range(group)]
        ql_b = [x.astype(BF16) for x in ql]
        qr_b = [x.astype(BF16) for x in qr]

        def chunk_body(c, carry):
            step = g * n_chunks + c
            slot = lax.rem(step, n_slots)
            issue(step + ahead)
            for cp in chunk_copies(step, slot):
                cp.wait()
            out = []
            for a in range(group):
                m, l, acc = carry[a]
                keys = pl.ds(a * kc, kc)
                cb = cbuf[slot, keys, :].astype(BF16)
                kb = kbuf[slot, :, keys].astype(BF16)
                s = _dot_nt(ql_b[a], cb) + _dot(qr_b[a], kb)
                m_new = jnp.maximum(m, jnp.max(s, axis=1, keepdims=True))
                alpha = jnp.exp2(m - m_new)
                p = jnp.exp2(s - m_new)
                l = alpha * l + jnp.sum(p, axis=1, keepdims=True)
                acc = alpha * acc + _dot(p.astype(BF16), cb)
                out.append((m_new, l, acc))
            return tuple(out)

        nh = ql[0].shape[0]
        init = tuple((jnp.full((nh, 1), -jnp.inf, F32), jnp.zeros((nh, 1), F32),
                      jnp.zeros(ql[0].shape, F32)) for _ in range(group))
        fin = lax.fori_loop(0, n_chunks, chunk_body, init)
        for a in range(group):
            m, l, acc = fin[a]
            b = g * group + a
            cn = cn_ref[b]
            kn = kn_ref[b]
            s_new = (jnp.sum(ql[a] * cn, axis=1, keepdims=True)
                     + jnp.sum(qr[a] * kn, axis=1, keepdims=True))
            m_f = jnp.maximum(m, s_new)
            alpha = jnp.exp2(m - m_f)
            p_new = jnp.exp2(s_new - m_f)
            o_ref[b] = (alpha * acc + p_new * cn) / (alpha * l + p_new)
        return 0

    lax.fori_loop(0, n_seq // group, group_body, 0)
    for step in range(total, total + ahead):
        for cp in chunk_copies(total - 1, step % n_slots):
            cp.wait()


def _mla_decode(page_table, q_lat, q_rope, ckv_new, kr_new, cache_ckv, cache_kr_t, layer,
                pages_per_chunk, group=2, n_slots=3):
    n_seq, n_pages = page_table.shape
    page_size, r = cache_ckv.shape[2], cache_ckv.shape[3]
    rope = cache_kr_t.shape[2]
    assert n_seq % group == 0 and n_pages % pages_per_chunk == 0
    n_chunks = n_pages // pages_per_chunk
    kc = pages_per_chunk * page_size
    kern = functools.partial(_mla_decode_kernel, layer=layer, n_seq=n_seq, n_chunks=n_chunks,
                             pages_per_chunk=pages_per_chunk, page_size=page_size,
                             group=group, n_slots=n_slots)
    vmem = pl.BlockSpec(memory_space=pltpu.VMEM)
    return pl.pallas_call(
        kern,
        out_shape=jax.ShapeDtypeStruct(q_lat.shape, F32),
        in_specs=[pl.BlockSpec(memory_space=pltpu.SMEM), vmem, vmem, vmem, vmem,
                  pl.BlockSpec(memory_space=pl.ANY), pl.BlockSpec(memory_space=pl.ANY)],
        out_specs=vmem,
        scratch_shapes=[pltpu.VMEM((n_slots, group * kc, r), F32),
                        pltpu.VMEM((n_slots, rope, group * kc), F32),
                        pltpu.SemaphoreType.DMA((2, n_slots))],
        compiler_params=pltpu.CompilerParams(vmem_limit_bytes=VMEM_LIMIT),
        name="mla_decode",
    )(page_table.reshape(-1), q_lat, q_rope, ckv_new, kr_new, cache_ckv, cache_kr_t)


def _out_proj_kernel(a_ref, h_ref, w1_ref, w2_ref, x_ref, g_ref, o_ref, hn_ref, *, tn):
    a = a_ref[...]
    h = h_ref[...]
    for c in range(o_ref.shape[1] // tn):
        cols = pl.ds(c * tn, tn)
        o_ref[:, cols] = x_ref[:, cols] + _dot(a, w1_ref[:, cols]) + _dot(h, w2_ref[:, cols])
    hn_ref[...] = _rms(o_ref[...], g_ref[...]).astype(hn_ref.dtype)


def _out_proj(a, h, w_out, x, g_next, tm, tn):
    t, d = x.shape
    ka = a.shape[1]
    kh = h.shape[1]
    assert ka == kh
    return pl.pallas_call(
        functools.partial(_out_proj_kernel, tn=tn),
        out_shape=(jax.ShapeDtypeStruct((t, d), F32), jax.ShapeDtypeStruct((t, d), BF16)),
        grid=(t // tm,),
        in_specs=[pl.BlockSpec((tm, ka), lambda i: (i, 0)),
                  pl.BlockSpec((tm, kh), lambda i: (i, 0)),
                  pl.BlockSpec((ka, d), lambda i: (0, 0)),
                  pl.BlockSpec((kh, d), lambda i: (1, 0)),
                  pl.BlockSpec((tm, d), lambda i: (i, 0)),
                  pl.BlockSpec((1, d), lambda i: (0, 0))],
        out_specs=(pl.BlockSpec((tm, d), lambda i: (i, 0)),
                   pl.BlockSpec((tm, d), lambda i: (i, 0))),
        compiler_params=_cparams(("arbitrary",)),
        name="out_proj",
    )(a, h, w_out, w_out, x, g_next)


def _ffn_up_kernel(*refs, seq_tiles, decode, row_split):
    if decode:
        (hn_ref, wg_ref, wu_ref, wc_ref, bc_ref, h0_ref, h1_ref,
         a_ref, u_ref, wgb_ref, wub_ref) = refs
    else:
        (hn_ref, wg_ref, wu_ref, wc_ref, bc_ref,
         a_ref, u_ref, wgb_ref, wub_ref, prev_ref) = refs
    i = pl.program_id(1)

    @pl.when(i == 0)
    def _():
        wgb_ref[...] = wg_ref[...].astype(BF16)
        wub_ref[...] = wu_ref[...].astype(BF16)

    wg = wgb_ref[...]
    wu = wub_ref[...]
    wc = wc_ref[...]
    bc = bc_ref[...]
    tm = hn_ref.shape[0]
    tail = u_ref.shape[0]
    if decode:
        u = _dot(hn_ref[...], wg)
        up = _dot(hn_ref[...], wu)
        u_ref[...] = u
        conv = h0_ref[...] * wc[0:1, :] + h1_ref[...] * wc[1:2, :] + u * wc[2:3, :] + bc
        a_ref[...] = (conv * _sigmoid(conv) * up).astype(a_ref.dtype)
        return

    @pl.when(i == 0)
    def _():
        prev_ref[...] = jnp.zeros(prev_ref.shape, F32)

    prev = jnp.where(lax.rem(i, seq_tiles) != 0, prev_ref[...], 0.0)
    sub = tm // row_split
    for r in range(row_split):
        rows = pl.ds(r * sub, sub)
        hn = hn_ref[rows, :]
        u = _dot(hn, wg)
        up = _dot(hn, wu)
        row = lax.broadcasted_iota(jnp.int32, u.shape, 0)
        p6 = jnp.broadcast_to(prev[6:7, :], u.shape)
        p7 = jnp.broadcast_to(prev[7:8, :], u.shape)
        u_m1 = jnp.where(row == 0, p7, pltpu.roll(u, 1, axis=0))
        u_m2 = jnp.where(row == 0, p6, jnp.where(row == 1, p7, pltpu.roll(u, 2, axis=0)))
        conv = u_m2 * wc[0:1, :] + u_m1 * wc[1:2, :] + u * wc[2:3, :] + bc
        a_ref[rows, :] = (conv * _sigmoid(conv) * up).astype(a_ref.dtype)
        prev = u[sub - 8:, :]
    prev_ref[...] = prev
    u_ref[...] = prev[8 - tail:, :]


def _ffn_up(hn, w_gate, w_up, w_conv, b_conv, tm, tf, seq=None, hist=None):
    t, d = hn.shape
    ff = w_gate.shape[1]
    decode = hist is not None
    tail = tm if decode else 8
    kern = functools.partial(_ffn_up_kernel, seq_tiles=None if decode else seq // tm,
                             decode=decode, row_split=4)
    in_specs = [pl.BlockSpec((tm, d), lambda f, i: (i, 0)),
                pl.BlockSpec((d, tf), lambda f, i: (0, f)),
                pl.BlockSpec((d, tf), lambda f, i: (0, f)),
                pl.BlockSpec((CONV_W, tf), lambda f, i: (0, f)),
                pl.BlockSpec((1, tf), lambda f, i: (0, f))]
    args = [hn, w_gate, w_up, w_conv, b_conv]
    scratch = [pltpu.VMEM((d, tf), BF16), pltpu.VMEM((d, tf), BF16)]
    if decode:
        in_specs += [pl.BlockSpec((tm, tf), lambda f, i: (i, f))] * 2
        args += list(hist)
    else:
        scratch.append(pltpu.VMEM((8, tf), F32))
    return pl.pallas_call(
        kern,
        out_shape=(jax.ShapeDtypeStruct((t, ff), BF16),
                   jax.ShapeDtypeStruct((t // tm * tail, ff), F32)),
        grid=(ff // tf, t // tm),
        in_specs=in_specs,
        out_specs=(pl.BlockSpec((tm, tf), lambda f, i: (i, f)),
                   pl.BlockSpec((tail, tf), lambda f, i: (i, f))),
        scratch_shapes=scratch,
        compiler_params=_cparams(("arbitrary", "arbitrary")),
        name="ffn_up",
    )(*args)


def _ffn_down_kernel(a_ref, w_ref, x_ref, g_ref, y_ref, *, tn):
    k = pl.program_id(1)

    @pl.when(k == 0)
    def _():
        y_ref[...] = x_ref[...]

    a = a_ref[...]
    for c in range(y_ref.shape[1] // tn):
        cols = pl.ds(c * tn, tn)
        y_ref[:, cols] += _dot(a, w_ref[:, cols].astype(BF16))

    @pl.when(k == pl.num_programs(1) - 1)
    def _():
        y_ref[...] = _rms(y_ref[...], g_ref[...])


def _ffn_down(a, w_down, x, g_final, tm, tk, tn):
    t, d = x.shape
    ff = a.shape[1]
    return pl.pallas_call(
        functools.partial(_ffn_down_kernel, tn=tn),
        out_shape=jax.ShapeDtypeStruct((t, d), F32),
        grid=(t // tm, ff // tk),
        in_specs=[pl.BlockSpec((tm, tk), lambda i, k: (i, k)),
                  pl.BlockSpec((tk, d), lambda i, k: (k, 0)),
                  pl.BlockSpec((tm, d), lambda i, k: (i, 0)),
                  pl.BlockSpec((1, d), lambda i, k: (0, 0))],
        out_specs=pl.BlockSpec((tm, d), lambda i, k: (i, 0)),
        compiler_params=_cparams(("arbitrary", "arbitrary")),
        name="ffn_down",
    )(a, w_down, x, g_final)


def _rope_tables(pos):
    inv = ROPE_THETA ** (-jnp.arange(0, MLA_ROPE, 2, dtype=F32) / MLA_ROPE)
    ang = pos.astype(F32)[:, None] * inv[None, :]
    cos, sin = jnp.cos(ang), jnp.sin(ang)
    zero = jnp.zeros_like(cos)
    return (jnp.concatenate([cos, cos, zero, zero], axis=1),
            jnp.concatenate([-sin, sin, zero, zero], axis=1))


def _prep_w_in(w_in, q_lora, kv_rank, dk, dv):
    nh = ML_HEADS
    o_kr = q_lora + kv_rank
    o_qm = o_kr + MLA_ROPE
    o_gate = o_qm + 2 * nh * dk + 2 * nh * dv
    half = MLA_ROPE // 2
    w_t = w_in.T.astype(BF16)
    pad = jnp.zeros((LANES - 2 * nh, w_in.shape[0]), BF16)
    rows = [w_t[:o_kr], w_t[o_qm:o_gate],
            w_t[o_kr:o_qm], w_t[o_kr + half:o_qm], w_t[o_kr:o_kr + half],
            w_t[o_gate:], pad]
    return jnp.concatenate(rows, axis=0)


def _prep_w_uq(w_uq):
    hd = MLA_NOPE + MLA_ROPE
    half = MLA_ROPE // 2
    cols = []
    for h in range(MLA_HEADS):
        o = h * hd
        cols += [w_uq[:, o:o + hd], w_uq[:, o + MLA_NOPE + half:o + hd],
                 w_uq[:, o + MLA_NOPE:o + MLA_NOPE + half]]
    return jnp.concatenate(cols, axis=1).astype(BF16)


def _layer(x, pos, lw, prep, mode, state):
    (g_mix, w_in, g_q, w_uq, g_kv, w_uk, w_uv, b_i, b_f, g_ml, w_out,
     g_ffn, w_gate, w_up, w_conv, b_conv, w_down) = lw
    w_in_p, w_uq_p, w_uk_f, w_uv_f, w_out_b = prep
    t, d = x.shape
    nh = ML_HEADS
    dk = (w_in.shape[1] - g_q.shape[0] - g_kv.shape[0] - MLA_ROPE - 2 * nh) // (6 * nh)
    dv = 2 * dk
    prompt = mode == "prompt"
    tm = 512 if prompt else t
    cos, sin = _rope_tables(pos)
    if not prompt:
        cos = jnp.broadcast_to(cos, (t, LANES))
        sin = jnp.broadcast_to(sin, (t, LANES))

    z = _proj_in(x, g_mix[None, :], w_in_p, tm, w_in_p.shape[0] // 2)
    q_cat, k_cat, v, ckv, kr = _mla_proj(
        z, g_q[None, :], g_kv[None, :], w_uq_p, w_uk_f.astype(BF16), w_uv_f.T.astype(BF16),
        cos, sin, tm)
    bias_r = jnp.concatenate([b_i, b_f])[None, :].astype(F32)

    if prompt:
        batch, seq = state["batch"], state["seq"]
        o_mla = _flash(q_cat, k_cat, v, batch, seq, min(seq, 1024), 2)
        gates_row = z[:, 33 * LANES:33 * LANES + 2 * nh].T
        h_ml, c_new, n_new, m_new = _mlstm_prompt(
            z, gates_row, bias_r, bias_r.T, g_ml[None, :], batch, seq, math.gcd(seq, 256), dk, dv)
        n_new = n_new[:, :, 0, :]
        m_new = m_new[:, :, 0, 0]
    else:
        r = g_kv.shape[0]
        q_lat = _q_absorb(q_cat, w_uk_f).reshape(t, MLA_HEADS, r)
        q_rope = q_cat.reshape(t, MLA_HEADS, QK_SLOT)[:, :, MLA_NOPE:MLA_NOPE + MLA_ROPE].astype(F32)
        o_lat = _mla_decode(state["page_table"], q_lat, q_rope, ckv[:, None, :], kr[:, None, :],
                            state["cache_ckv"], state["cache_kr_t"], state["layer"], 16)
        o_mla = _o_absorb(o_lat.reshape(t, MLA_HEADS * r), w_uv_f)
        h_ml, c_new, n_new, m_new = _mlstm_sample(
            z[:, None, :], bias_r, g_ml[None, :], state["C"], state["n"], state["m"][:, None, :],
            dk, dv, 4)
        h_ml = h_ml[:, 0, :]
        m_new = m_new[:, 0, :]

    x1, hn = _out_proj(o_mla, h_ml, w_out_b, x, g_ffn[None, :], min(t, 512), 512)
    if prompt:
        tmf = 1024
        a, u_tail = _ffn_up(hn, w_gate, w_up, w_conv, b_conv[None, :], tmf, 512, seq=seq)
        tiles = seq // tmf
        u_tail = u_tail.reshape(batch, tiles, 8, -1)
        conv_new = u_tail[:, tiles - 1, 8 - (CONV_W - 1):, :]
    else:
        conv0 = state["conv"]
        a, u = _ffn_up(hn, w_gate, w_up, w_conv, b_conv[None, :], t, 512,
                       hist=(conv0[:, 0, :], conv0[:, 1, :]))
        conv_new = jnp.stack([conv0[:, 1, :], u], axis=1)
    y = _ffn_down(a, w_down, x1, state["g_out"][None, :], min(t, 1024), 512, 512)
    return y, (ckv, kr, c_new, n_new, m_new, conv_new)


def kernel(x_prompt, x_sample, cache_ckv, cache_kr, state_C, state_n, state_m, state_conv, page_table, g_mix, w_in, g_q, w_uq, g_kv, w_uk, w_uv, b_i, b_f, g_ml, w_out, g_ffn, w_gate, w_up, w_conv, b_conv, w_down, g_final):
    depth = w_in.shape[0]
    assert depth == 1, "the final RMSNorm is fused into the (single) layer's FFN kernel"
    bp, sp, d = x_prompt.shape
    bs, ss, _ = x_sample.shape
    assert ss == 1, "the sample group decodes one token per sequence"
    past_len = page_table.shape[1] * cache_ckv.shape[2]
    pos_p = jnp.arange(sp)
    pos_s = past_len + jnp.arange(ss)
    l = 0
    lw = (g_mix[l], w_in[l], g_q[l], w_uq[l], g_kv[l], w_uk[l], w_uv[l], b_i[l], b_f[l],
          g_ml[l], w_out[l], g_ffn[l], w_gate[l], w_up[l], w_conv[l], b_conv[l], w_down[l])
    q_lora, kv_rank = g_q.shape[1], g_kv.shape[1]
    nh = ML_HEADS
    dk = (w_in.shape[2] - q_lora - kv_rank - MLA_ROPE - 2 * nh) // (6 * nh)
    prep = (_prep_w_in(w_in[l], q_lora, kv_rank, dk, 2 * dk), _prep_w_uq(w_uq[l]),
            w_uk[l].reshape(kv_rank, -1), w_uv[l].reshape(kv_rank, -1), w_out[l].astype(BF16))
    cache_kr_t = jnp.swapaxes(cache_kr, 2, 3)

    yp, new_p = _layer(x_prompt.reshape(bp * sp, d), pos_p, lw, prep, "prompt",
                       dict(batch=bp, seq=sp, g_out=g_final))
    ys, new_s = _layer(x_sample.reshape(bs * ss, d), pos_s, lw, prep, "sample",
                       dict(page_table=page_table, cache_ckv=cache_ckv, cache_kr_t=cache_kr_t,
                            layer=l, C=state_C[l], n=state_n[l], m=state_m[l],
                            conv=state_conv[l], g_out=g_final))
    ckv_p, kr_p, c_p, n_p, m_p, conv_p = new_p
    ckv_s, kr_s, c_s, n_s, m_s, conv_s = new_s
    return (yp.reshape(bp, sp, d), ys.reshape(bs, ss, d),
            ckv_p.reshape(1, bp, sp, -1), kr_p.reshape(1, bp, sp, -1),
            c_p[None], n_p[None], m_p[None], conv_p[None],
            ckv_s.reshape(1, bs, ss, -1), kr_s.reshape(1, bs, ss, -1),
            c_s[None], n_s[None], m_s[None], conv_s[None])
```

```python
import functools
import math

import jax
import jax.numpy as jnp
from jax import lax
from jax.experimental import pallas as pl
from jax.experimental.pallas import tpu as pltpu

F32 = jnp.float32
BF16 = jnp.bfloat16

MLA_HEADS = 8
MLA_NOPE = 128
MLA_ROPE = 64
MLA_V = 128
ML_HEADS = 4
CONV_W = 3
ROPE_THETA = 10000.0
EPS = 1e-6
SCALE = (MLA_NOPE + MLA_ROPE) ** -0.5
LOG2E = math.log2(math.e)

LANES = 128
QK_SLOT = 2 * LANES
VMEM_LIMIT = 56 * 1024 * 1024


def _cparams(sem, vmem=VMEM_LIMIT):
    return pltpu.CompilerParams(dimension_semantics=sem, vmem_limit_bytes=vmem)


def _rms(x, g):
    return x * lax.rsqrt(jnp.mean(x * x, axis=-1, keepdims=True) + EPS) * g


def _sigmoid(x):
    return 1.0 / (1.0 + jnp.exp(-x))


def _log_sigmoid(x):
    return jnp.minimum(x, 0.0) - jnp.log1p(jnp.exp(-jnp.abs(x)))


def _dot(a, b):
    return jnp.dot(a, b, preferred_element_type=F32)


def _dot_nt(a, b):
    return lax.dot_general(a, b, (((1,), (1,)), ((), ())), preferred_element_type=F32)


def _dot_tn(a, b):
    return lax.dot_general(a, b, (((0,), (0,)), ((), ())), preferred_element_type=F32)


def _proj_in_kernel(x_ref, g_ref, w_ref, z_ref, xn_ref):
    @pl.when(pl.program_id(1) == 0)
    def _():
        xn_ref[...] = _rms(x_ref[...], g_ref[...]).astype(BF16)

    z_ref[...] = _dot_nt(xn_ref[...], w_ref[...])


def _proj_in(x, g, w_t, tm, tn):
    t, d = x.shape
    n = w_t.shape[0]
    return pl.pallas_call(
        _proj_in_kernel,
        out_shape=jax.ShapeDtypeStruct((t, n), F32),
        grid=(t // tm, n // tn),
        in_specs=[pl.BlockSpec((tm, d), lambda i, j: (i, 0)),
                  pl.BlockSpec((1, d), lambda i, j: (0, 0)),
                  pl.BlockSpec((tn, d), lambda i, j: (j, 0))],
        out_specs=pl.BlockSpec((tm, tn), lambda i, j: (i, j)),
        scratch_shapes=[pltpu.VMEM((tm, d), BF16)],
        compiler_params=_cparams(("arbitrary", "arbitrary")),
        name="proj_in",
    )(x, g, w_t)


def _rope128(x, cos, sin):
    return x * cos + pltpu.roll(x, 64, axis=1) * sin


def _mla_proj_kernel(cq_ref, ckv_ref, tail_ref, gq_ref, gkv_ref, wq_ref, wk_ref, wv_ref,
                     cos_ref, sin_ref, q_ref, k_ref, v_ref, ckv_out_ref, kr_out_ref):
    cos = cos_ref[...]
    sin = sin_ref[...]
    cqn = _rms(cq_ref[...], gq_ref[...]).astype(BF16)
    q = _dot(cqn, wq_ref[...]) * (SCALE * LOG2E)
    ckv = _rms(ckv_ref[...], gkv_ref[...])
    ckv_out_ref[...] = ckv
    ckv_b = ckv.astype(BF16)
    kn = _dot(ckv_b, wk_ref[...])
    v_ref[...] = _dot_nt(wv_ref[...], ckv_b).astype(BF16)
    kr = _rope128(tail_ref[...], cos, sin)
    kr_out_ref[...] = kr[:, :MLA_ROPE]
    kr_b = kr.astype(BF16)
    for h in range(MLA_HEADS):
        q0 = h * QK_SLOT
        q_ref[:, q0:q0 + LANES] = q[:, q0:q0 + LANES].astype(BF16)
        q_ref[:, q0 + LANES:q0 + QK_SLOT] = _rope128(
            q[:, q0 + LANES:q0 + QK_SLOT], cos, sin).astype(BF16)
        k_ref[:, q0:q0 + LANES] = kn[:, h * MLA_NOPE:(h + 1) * MLA_NOPE].astype(BF16)
        k_ref[:, q0 + LANES:q0 + QK_SLOT] = kr_b


def _mla_proj(z, g_q, g_kv, wq, wk, wv_t, cos, sin, tm):
    wv = wv_t
    t = z.shape[0]
    rows = cos.shape[0] // tm
    hq = MLA_HEADS * QK_SLOT
    qlr, kvr = wq.shape[0], wk.shape[0]
    assert qlr == kvr, "c_q and c_kv are addressed as equal-width column blocks of z"
    tail_blk = (z.shape[1] - 2 * LANES) // LANES
    const = lambda i: (0, 0)
    return pl.pallas_call(
        _mla_proj_kernel,
        out_shape=(jax.ShapeDtypeStruct((t, hq), BF16),
                   jax.ShapeDtypeStruct((t, hq), BF16),
                   jax.ShapeDtypeStruct((MLA_HEADS * MLA_V, t), BF16),
                   jax.ShapeDtypeStruct((t, kvr), F32),
                   jax.ShapeDtypeStruct((t, MLA_ROPE), F32)),
        grid=(t // tm,),
        in_specs=[pl.BlockSpec((tm, qlr), lambda i: (i, 0)),
                  pl.BlockSpec((tm, kvr), lambda i: (i, 1)),
                  pl.BlockSpec((tm, LANES), lambda i: (i, tail_blk)),
                  pl.BlockSpec((1, qlr), const),
                  pl.BlockSpec((1, kvr), const),
                  pl.BlockSpec(wq.shape, const),
                  pl.BlockSpec(wk.shape, const),
                  pl.BlockSpec(wv.shape, const),
                  pl.BlockSpec((tm, LANES), lambda i: (i % rows, 0)),
                  pl.BlockSpec((tm, LANES), lambda i: (i % rows, 0))],
        out_specs=(pl.BlockSpec((tm, hq), lambda i: (i, 0)),
                   pl.BlockSpec((tm, hq), lambda i: (i, 0)),
                   pl.BlockSpec((MLA_HEADS * MLA_V, tm), lambda i: (0, i)),
                   pl.BlockSpec((tm, kvr), lambda i: (i, 0)),
                   pl.BlockSpec((tm, MLA_ROPE), lambda i: (i, 0))),
        compiler_params=_cparams(("arbitrary",)),
        name="mla_proj",
    )(z, z, z, g_q, g_kv, wq, wk, wv, cos, sin)


def _flash_kernel(q_ref, k_ref, vt_ref, o_ref, m_ref, l_ref, acc_ref, *, heads):
    qi = pl.program_id(2)
    tq = q_ref.shape[0]
    m_ref[...] = jnp.full(m_ref.shape, -jnp.inf, F32)
    l_ref[...] = jnp.zeros(l_ref.shape, F32)
    acc_ref[...] = jnp.zeros(acc_ref.shape, F32)

    def update(j, diagonal):
        keys = pl.ds(pl.multiple_of(j * tq, tq), tq)
        for h in range(heads):
            k = k_ref[keys, h * QK_SLOT:(h + 1) * QK_SLOT]
            vt = vt_ref[h * MLA_V:(h + 1) * MLA_V, keys]
            st = _dot_nt(k, q_ref[:, h * QK_SLOT:(h + 1) * QK_SLOT])
            if diagonal:
                key = lax.broadcasted_iota(jnp.int32, st.shape, 0)
                qry = lax.broadcasted_iota(jnp.int32, st.shape, 1)
                st = jnp.where(key <= qry, st, -jnp.inf)
            m_prev = m_ref[h]
            m_new = jnp.maximum(m_prev, jnp.max(st, axis=0, keepdims=True))
            alpha = jnp.exp2(m_prev - m_new)
            pt = jnp.exp2(st - m_new)
            l_ref[h] = alpha * l_ref[h] + jnp.sum(pt, axis=0, keepdims=True)
            acc_ref[h] = alpha * acc_ref[h] + _dot(vt, pt.astype(BF16))
            m_ref[h] = m_new

    def body(j, carry):
        update(j, False)
        return carry

    lax.fori_loop(0, qi, body, 0)
    update(qi, True)
    for h in range(heads):
        o_ref[:, h * MLA_V:(h + 1) * MLA_V] = (acc_ref[h] / l_ref[h]).T.astype(o_ref.dtype)


def _flash(q, k, v_t, batch, seq, tq, heads):
    nb = seq // tq
    t = q.shape[0]
    hg = MLA_HEADS // heads
    return pl.pallas_call(
        functools.partial(_flash_kernel, heads=heads),
        out_shape=jax.ShapeDtypeStruct((t, MLA_HEADS * MLA_V), BF16),
        grid=(batch, hg, nb),
        in_specs=[pl.BlockSpec((tq, heads * QK_SLOT), lambda b, h, i: (b * nb + i, h)),
                  pl.BlockSpec((seq, heads * QK_SLOT), lambda b, h, i: (b, h)),
                  pl.BlockSpec((heads * MLA_V, seq), lambda b, h, i: (h, b))],
        out_specs=pl.BlockSpec((tq, heads * MLA_V), lambda b, h, i: (b * nb + i, h)),
        scratch_shapes=[pltpu.VMEM((heads, 1, tq), F32), pltpu.VMEM((heads, 1, tq), F32),
                        pltpu.VMEM((heads, MLA_V, tq), F32)],
        compiler_params=_cparams(("arbitrary",) * 3),
        name="flash_attn",
    )(q, k, v_t)


def _mlstm_chunk_kernel(q_ref, k_ref, v_ref, o_ref, gc_ref, gr_ref, bias_r_ref, bias_c_ref,
                        gml_ref, h_ref, c_out_ref, n_out_ref, m_out_ref,
                        c_sc, n_sc, m_sc, *, dk, dv):
    c_idx = pl.program_id(1)
    L = q_ref.shape[0]
    nh = ML_HEADS

    @pl.when(c_idx == 0)
    def _():
        c_sc[...] = jnp.zeros(c_sc.shape, F32)
        n_sc[...] = jnp.zeros(n_sc.shape, F32)
        m_sc[...] = jnp.zeros(m_sc.shape, F32)

    gates_c = gc_ref[...][:, :2 * nh] + bias_r_ref[...]
    gates_r = gr_ref[...] + bias_c_ref[...]
    row = lax.broadcasted_iota(jnp.int32, (L, L), 0)
    col = lax.broadcasted_iota(jnp.int32, (L, L), 1)
    lower = col <= row
    lower_f = lower.astype(F32)
    upper_f = (row <= col).astype(F32)
    hp = lax.Precision.HIGHEST
    b_col = jnp.dot(lower_f, _log_sigmoid(gates_c[:, nh:]), precision=hp,
                    preferred_element_type=F32)
    b_row = jnp.dot(_log_sigmoid(gates_r[nh:, :]), upper_f, precision=hp,
                    preferred_element_type=F32)

    for h in range(nh):
        bc = b_col[:, h:h + 1]
        br = b_row[h:h + 1, :]
        ic = gates_c[:, h:h + 1]
        ir = gates_r[h:h + 1, :]
        m_prev = m_sc[h][0:1, 0:1]
        log_d = jnp.where(lower, bc - br + ir, -jnp.inf)
        log_prev = bc + m_prev
        m_t = jnp.maximum(log_prev, jnp.max(log_d, axis=1, keepdims=True))
        q = q_ref[:, h * dk:(h + 1) * dk]
        k = k_ref[:, h * dk:(h + 1) * dk] * (dk ** -0.5)
        v = v_ref[:, h * dv:(h + 1) * dv]
        qb = q.astype(BF16)
        kb = k.astype(BF16)
        s = _dot_nt(qb, kb) * jnp.exp(log_d - m_t)
        w_prev = jnp.exp(log_prev - m_t)
        c_prev = c_sc[h]
        n_prev = n_sc[h][0:1, :]
        num = _dot(s.astype(BF16), v.astype(BF16)) + w_prev * _dot_nt(qb, c_prev.astype(BF16))
        den = jnp.sum(s, axis=1, keepdims=True) + w_prev * jnp.sum(q * n_prev, axis=1, keepdims=True)
        hh = num / jnp.maximum(jnp.abs(den), jnp.exp(-m_t))
        hh = _rms(hh, gml_ref[:, h * dv:(h + 1) * dv])
        h_ref[:, h * dv:(h + 1) * dv] = (hh * _sigmoid(o_ref[:, h * dv:(h + 1) * dv])).astype(h_ref.dtype)
        m_new = m_t[L - 1:L, :]
        b_last = bc[L - 1:L, :]
        w_s = jnp.exp(b_last - bc + ic - m_new)
        decay = jnp.exp(b_last + m_prev - m_new)
        c_sc[h] = decay * c_prev + _dot_tn((w_s * v).astype(BF16), kb)
        n_sc[h] = jnp.broadcast_to(decay * n_prev + jnp.sum(w_s * k, axis=0, keepdims=True),
                                   n_sc.shape[1:])
        m_sc[h] = jnp.broadcast_to(m_new, m_sc.shape[1:])

    @pl.when(c_idx == pl.num_programs(1) - 1)
    def _():
        c_out_ref[0] = c_sc[...]
        n_out_ref[0] = n_sc[...]
        m_out_ref[0] = m_sc[...]


def _mlstm_prompt(z, gates_row, bias_r, bias_c, g_ml, batch, seq, chunk, dk, dv):
    t = z.shape[0]
    nc = seq // chunk
    nh = ML_HEADS
    kern = functools.partial(_mlstm_chunk_kernel, dk=dk, dv=dv)
    tok = lambda w, blk: pl.BlockSpec((chunk, w), lambda b, c: (b * nc + c, blk))
    const = lambda b, c: (0, 0)
    st = lambda b, c: (b, 0, 0, 0)
    return pl.pallas_call(
        kern,
        out_shape=(jax.ShapeDtypeStruct((t, nh * dv), BF16),
                   jax.ShapeDtypeStruct((batch, nh, dv, dk), F32),
                   jax.ShapeDtypeStruct((batch, nh, 8, dk), F32),
                   jax.ShapeDtypeStruct((batch, nh, 8, LANES), F32)),
        grid=(batch, nc),
        in_specs=[tok(nh * dk, 2), tok(nh * dk, 3), tok(nh * dv, 2), tok(nh * dv, 3),
                  tok(LANES, 33),
                  pl.BlockSpec((2 * nh, chunk), lambda b, c: (0, b * nc + c)),
                  pl.BlockSpec((1, 2 * nh), const),
                  pl.BlockSpec((2 * nh, 1), const),
                  pl.BlockSpec((1, nh * dv), const)],
        out_specs=(pl.BlockSpec((chunk, nh * dv), lambda b, c: (b * nc + c, 0)),
                   pl.BlockSpec((1, nh, dv, dk), st),
                   pl.BlockSpec((1, nh, 8, dk), st),
                   pl.BlockSpec((1, nh, 8, LANES), st)),
        scratch_shapes=[pltpu.VMEM((nh, dv, dk), F32), pltpu.VMEM((nh, 8, dk), F32),
                        pltpu.VMEM((nh, 8, LANES), F32)],
        compiler_params=_cparams(("arbitrary", "arbitrary")),
        name="mlstm_chunk",
    )(z, z, z, z, z, gates_row, bias_r, bias_c, g_ml)


def _mlstm_step_kernel(q_ref, k_ref, v_ref, o_ref, g_ref, bias_ref, gml_ref, c0_ref, n0_ref,
                       m0_ref, h_ref, c_ref, n_ref, m_ref, *, dk, dv):
    nh = ML_HEADS
    bd_lo = lax.broadcasted_iota(jnp.int32, (8, nh * dk), 0) * dk
    bd_col = lax.broadcasted_iota(jnp.int32, (8, nh * dk), 1)
    bd_mask = (bd_col >= bd_lo) & (bd_col < bd_lo + dk)
    v_row = lax.broadcasted_iota(jnp.int32, (8, dv), 0)
    for t in range(q_ref.shape[0]):
        gates = g_ref[t][:, :2 * nh] + bias_ref[...]
        m_all = m0_ref[t]
        n_all = n0_ref[t]
        q_all = q_ref[t]
        k_all = k_ref[t] * (dk ** -0.5)
        q_bd = jnp.where(bd_mask, jnp.broadcast_to(q_all, bd_mask.shape), 0.0).astype(BF16)
        k_bd = jnp.where(bd_mask, jnp.broadcast_to(k_all, bd_mask.shape), 0.0).astype(BF16)
        c_cat = jnp.concatenate([c0_ref[t, h] for h in range(nh)], axis=1)
        cq_all = _dot_nt(q_bd, c_cat.astype(BF16))
        m_cols = []
        wv_rows = jnp.zeros((8, dv), F32)
        decays = []
        for h in range(nh):
            i_g = gates[:, h:h + 1]
            b = _log_sigmoid(gates[:, nh + h:nh + h + 1])
            m_prev = m_all[:, h:h + 1]
            q = q_all[:, h * dk:(h + 1) * dk]
            k = k_all[:, h * dk:(h + 1) * dk]
            v = v_ref[t][:, h * dv:(h + 1) * dv]
            log_prev = b + m_prev
            m_t = jnp.maximum(log_prev, i_g)
            qk = jnp.sum(q.astype(BF16).astype(F32) * k.astype(BF16).astype(F32),
                         axis=1, keepdims=True)
            s = qk * jnp.exp(i_g - m_t)
            w_prev = jnp.exp(log_prev - m_t)
            n_prev = n_all[h:h + 1, :]
            num = s * v.astype(BF16).astype(F32) + w_prev * cq_all[h:h + 1, :]
            den = s + w_prev * jnp.sum(q * n_prev, axis=1, keepdims=True)
            hh = num / jnp.maximum(jnp.abs(den), jnp.exp(-m_t))
            hh = _rms(hh, gml_ref[:, h * dv:(h + 1) * dv])
            gate = _sigmoid(o_ref[t][:, h * dv:(h + 1) * dv])
            h_ref[t, :, h * dv:(h + 1) * dv] = (hh * gate).astype(h_ref.dtype)
            w_s = jnp.exp(i_g - m_t)
            decay = jnp.exp(log_prev - m_t)
            wv_rows = jnp.where(v_row == h, jnp.broadcast_to(w_s * v, (8, dv)), wv_rows)
            n_ref[t, h:h + 1, :] = decay * n_prev + w_s * k
            decays.append(decay)
            m_cols.append(m_t)
        outer = _dot_tn(wv_rows.astype(BF16), k_bd)
        for h in range(nh):
            c_ref[t, h] = decays[h] * c0_ref[t, h] + outer[:, h * dk:(h + 1) * dk]
        m_ref[t] = jnp.concatenate(m_cols, axis=1)


def _mlstm_sample(z3, bias_r, g_ml, c0, n0, m0, dk, dv, tb):
    nb = z3.shape[0]
    nh = ML_HEADS
    kern = functools.partial(_mlstm_step_kernel, dk=dk, dv=dv)
    tok = lambda w, blk: pl.BlockSpec((tb, 1, w), lambda b: (b, 0, blk))
    const = lambda b: (0, 0)
    return pl.pallas_call(
        kern,
        out_shape=(jax.ShapeDtypeStruct((nb, 1, nh * dv), BF16),
                   jax.ShapeDtypeStruct((nb, nh, dv, dk), F32),
                   jax.ShapeDtypeStruct((nb, nh, dk), F32),
                   jax.ShapeDtypeStruct((nb, 1, nh), F32)),
        grid=(nb // tb,),
        in_specs=[tok(nh * dk, 2), tok(nh * dk, 3), tok(nh * dv, 2), tok(nh * dv, 3),
                  tok(LANES, 33),
                  pl.BlockSpec((1, 2 * nh), const),
                  pl.BlockSpec((1, nh * dv), const),
                  pl.BlockSpec((tb, nh, dv, dk), lambda b: (b, 0, 0, 0)),
                  pl.BlockSpec((tb, nh, dk), lambda b: (b, 0, 0)),
                  pl.BlockSpec((tb, 1, nh), lambda b: (b, 0, 0))],
        out_specs=(pl.BlockSpec((tb, 1, nh * dv), lambda b: (b, 0, 0)),
                   pl.BlockSpec((tb, nh, dv, dk), lambda b: (b, 0, 0, 0)),
                   pl.BlockSpec((tb, nh, dk), lambda b: (b, 0, 0)),
                   pl.BlockSpec((tb, 1, nh), lambda b: (b, 0, 0))),
        compiler_params=_cparams(("arbitrary",)),
        name="mlstm_step",
    )(z3, z3, z3, z3, z3, bias_r, g_ml, c0, n0, m0)


def _q_absorb_kernel(q_ref, wk_ref, o_ref):
    o_ref[...] = _dot_nt(q_ref[:, :MLA_NOPE], wk_ref[...].astype(BF16))


def _q_absorb(q_cat, w_uk_flat):
    t = q_cat.shape[0]
    r = w_uk_flat.shape[0]
    return pl.pallas_call(
        _q_absorb_kernel,
        out_shape=jax.ShapeDtypeStruct((t, MLA_HEADS * r), F32),
        grid=(MLA_HEADS,),
        in_specs=[pl.BlockSpec((t, QK_SLOT), lambda h: (0, h)),
                  pl.BlockSpec((r, MLA_NOPE), lambda h: (0, h))],
        out_specs=pl.BlockSpec((t, r), lambda h: (0, h)),
        compiler_params=_cparams(("arbitrary",)),
        name="q_absorb",
    )(q_cat, w_uk_flat)


def _o_absorb_kernel(o_ref, wv_ref, out_ref):
    out_ref[...] = _dot(o_ref[...].astype(BF16), wv_ref[...].astype(BF16)).astype(out_ref.dtype)


def _o_absorb(o_lat2d, w_uv_flat):
    t = o_lat2d.shape[0]
    r = w_uv_flat.shape[0]
    return pl.pallas_call(
        _o_absorb_kernel,
        out_shape=jax.ShapeDtypeStruct((t, MLA_HEADS * MLA_V), BF16),
        grid=(MLA_HEADS,),
        in_specs=[pl.BlockSpec((t, r), lambda h: (0, h)),
                  pl.BlockSpec((r, MLA_V), lambda h: (0, h))],
        out_specs=pl.BlockSpec((t, MLA_V), lambda h: (0, h)),
        compiler_params=_cparams(("arbitrary",)),
        name="o_absorb",
    )(o_lat2d, w_uv_flat)


def _mla_decode_kernel(pt_ref, ql_ref, qr_ref, cn_ref, kn_ref, cc_hbm, ckt_hbm, o_ref,
                       cbuf, kbuf, sem, *, layer, n_seq, n_chunks, pages_per_chunk, page_size,
                       group, n_slots):
    total = (n_seq // group) * n_chunks
    pages_per_seq = n_chunks * pages_per_chunk
    kc = pages_per_chunk * page_size
    ahead = n_slots - 1

    def chunk_copies(step, slot):
        g = step // n_chunks
        c = step - g * n_chunks
        cps = []
        for a in range(group):
            base = (g * group + a) * pages_per_seq + c * pages_per_chunk
            for j in range(pages_per_chunk):
                page = pt_ref[base + j]
                dst = pl.ds(a * kc + j * page_size, page_size)
                cps.append(pltpu.make_async_copy(
                    cc_hbm.at[layer, page], cbuf.at[slot, dst], sem.at[0, slot]))
                cps.append(pltpu.make_async_copy(
                    ckt_hbm.at[layer, page], kbuf.at[slot, :, dst], sem.at[1, slot]))
        return cps

    def issue(step):
        if isinstance(step, int):
            src, slot = min(step, total - 1), step % n_slots
        else:
            src, slot = jnp.minimum(step, total - 1), lax.rem(step, n_slots)
        for cp in chunk_copies(src, slot):
            cp.start()

    for step in range(ahead):
        issue(step)

    def group_body(g, _):
        ql = [ql_ref[g * group + a] for a in range(group)]
        qr = [qr_ref[g * group + a] for a in range(group)]
        ql_b = [x.astype(BF16) for x in ql]
        qr_b = [x.astype(BF16) for x in qr]

        def chunk_body(c, carry):
            step = g * n_chunks + c
            slot = lax.rem(step, n_slots)
            issue(step + ahead)
            for cp in chunk_copies(step, slot):
                cp.wait()
            out = []
            for a in range(group):
                m, l, acc = carry[a]
                keys = pl.ds(a * kc, kc)
                cb = cbuf[slot, keys, :].astype(BF16)
                kb = kbuf[slot, :, keys].astype(BF16)
                s = _dot_nt(ql_b[a], cb) + _dot(qr_b[a], kb)
                m_new = jnp.maximum(m, jnp.max(s, axis=1, keepdims=True))
                alpha = jnp.exp2(m - m_new)
                p = jnp.exp2(s - m_new)
                l = alpha * l + jnp.sum(p, axis=1, keepdims=True)
                acc = alpha * acc + _dot(p.astype(BF16), cb)
                out.append((m_new, l, acc))
            return tuple(out)

        nh = ql[0].shape[0]
        init = tuple((jnp.full((nh, 1), -jnp.inf, F32), jnp.zeros((nh, 1), F32),
                      jnp.zeros(ql[0].shape, F32)) for _ in range(group))
        fin = lax.fori_loop(0, n_chunks, chunk_body, init)
        for a in range(group):
            m, l, acc = fin[a]
            b = g * group + a
            cn = cn_ref[b]
            kn = kn_ref[b]
            s_new = (jnp.sum(ql[a] * cn, axis=1, keepdims=True)
                     + jnp.sum(qr[a] * kn, axis=1, keepdims=True))
            m_f = jnp.maximum(m, s_new)
            alpha = jnp.exp2(m - m_f)
            p_new = jnp.exp2(s_new - m_f)
            o_ref[b] = (alpha * acc + p_new * cn) / (alpha * l + p_new)
        return 0

    lax.fori_loop(0, n_seq // group, group_body, 0)
    for step in range(total, total + ahead):
        for cp in chunk_copies(total - 1, step % n_slots):
            cp.wait()


def _mla_decode(page_table, q_lat, q_rope, ckv_new, kr_new, cache_ckv, cache_kr_t, layer,
                pages_per_chunk, group=2, n_slots=3):
    n_seq, n_pages = page_table.shape
    page_size, r = cache_ckv.shape[2], cache_ckv.shape[3]
    rope = cache_kr_t.shape[2]
    assert n_seq % group == 0 and n_pages % pages_per_chunk == 0
    n_chunks = n_pages // pages_per_chunk
    kc = pages_per_chunk * page_size
    kern = functools.partial(_mla_decode_kernel, layer=layer, n_seq=n_seq, n_chunks=n_chunks,
                             pages_per_chunk=pages_per_chunk, page_size=page_size,
                             group=group, n_slots=n_slots)
    vmem = pl.BlockSpec(memory_space=pltpu.VMEM)
    return pl.pallas_call(
        kern,
        out_shape=jax.ShapeDtypeStruct(q_lat.shape, F32),
        in_specs=[pl.BlockSpec(memory_space=pltpu.SMEM), vmem, vmem, vmem, vmem,
                  pl.BlockSpec(memory_space=pl.ANY), pl.BlockSpec(memory_space=pl.ANY)],
        out_specs=vmem,
        scratch_shapes=[pltpu.VMEM((n_slots, group * kc, r), F32),
                        pltpu.VMEM((n_slots, rope, group * kc), F32),
                        pltpu.SemaphoreType.DMA((2, n_slots))],
        compiler_params=pltpu.CompilerParams(vmem_limit_bytes=VMEM_LIMIT),
        name="mla_decode",
    )(page_table.reshape(-1), q_lat, q_rope, ckv_new, kr_new, cache_ckv, cache_kr_t)


def _out_proj_kernel(a_ref, h_ref, w1_ref, w2_ref, x_ref, g_ref, o_ref, hn_ref, *, tn):
    a = a_ref[...]
    h = h_ref[...]
    for c in range(o_ref.shape[1] // tn):
        cols = pl.ds(c * tn, tn)
        o_ref[:, cols] = x_ref[:, cols] + _dot(a, w1_ref[:, cols]) + _dot(h, w2_ref[:, cols])
    hn_ref[...] = _rms(o_ref[...], g_ref[...]).astype(hn_ref.dtype)


def _out_proj(a, h, w_out, x, g_next, tm, tn):
    t, d = x.shape
    ka = a.shape[1]
    kh = h.shape[1]
    assert ka == kh
    return pl.pallas_call(
        functools.partial(_out_proj_kernel, tn=tn),
        out_shape=(jax.ShapeDtypeStruct((t, d), F32), jax.ShapeDtypeStruct((t, d), BF16)),
        grid=(t // tm,),
        in_specs=[pl.BlockSpec((tm, ka), lambda i: (i, 0)),
                  pl.BlockSpec((tm, kh), lambda i: (i, 0)),
                  pl.BlockSpec((ka, d), lambda i: (0, 0)),
                  pl.BlockSpec((kh, d), lambda i: (1, 0)),
                  pl.BlockSpec((tm, d), lambda i: (i, 0)),
                  pl.BlockSpec((1, d), lambda i: (0, 0))],
        out_specs=(pl.BlockSpec((tm, d), lambda i: (i, 0)),
                   pl.BlockSpec((tm, d), lambda i: (i, 0))),
        compiler_params=_cparams(("arbitrary",)),
        name="out_proj",
    )(a, h, w_out, w_out, x, g_next)


def _ffn_up_kernel(hn_ref, wg_ref, wu_ref, wc_ref, bc_ref, hs_ref, h0_ref, h1_ref,
                   a_ref, u_ref, as_ref, us_ref, wgb_ref, wub_ref, prev_ref,
                   *, seq_tiles, row_split):
    i = pl.program_id(1)
    wc = wc_ref[...]
    bc = bc_ref[...]

    @pl.when(i == 0)
    def _():
        wgb_ref[...] = wg_ref[...].astype(BF16)
        wub_ref[...] = wu_ref[...].astype(BF16)
        prev_ref[...] = jnp.zeros(prev_ref.shape, F32)
        u = _dot(hs_ref[...], wgb_ref[...])
        up = _dot(hs_ref[...], wub_ref[...])
        us_ref[...] = u
        conv = h0_ref[...] * wc[0:1, :] + h1_ref[...] * wc[1:2, :] + u * wc[2:3, :] + bc
        as_ref[...] = (conv * _sigmoid(conv) * up).astype(as_ref.dtype)

    wg = wgb_ref[...]
    wu = wub_ref[...]
    tm = hn_ref.shape[0]
    tail = u_ref.shape[0]
    prev = jnp.where(lax.rem(i, seq_tiles) != 0, prev_ref[...], 0.0)
    sub = tm // row_split
    for r in range(row_split):
        rows = pl.ds(r * sub, sub)
        hn = hn_ref[rows, :]
        u = _dot(hn, wg)
        up = _dot(hn, wu)
        row = lax.broadcasted_iota(jnp.int32, u.shape, 0)
        p6 = jnp.broadcast_to(prev[6:7, :], u.shape)
        p7 = jnp.broadcast_to(prev[7:8, :], u.shape)
        u_m1 = jnp.where(row == 0, p7, pltpu.roll(u, 1, axis=0))
        u_m2 = jnp.where(row == 0, p6, jnp.where(row == 1, p7, pltpu.roll(u, 2, axis=0)))
        conv = u_m2 * wc[0:1, :] + u_m1 * wc[1:2, :] + u * wc[2:3, :] + bc
        a_ref[rows, :] = (conv * _sigmoid(conv) * up).astype(a_ref.dtype)
        prev = u[sub - 8:, :]
    prev_ref[...] = prev
    u_ref[...] = prev[8 - tail:, :]


def _ffn_up(hn, hs, hist, w_gate, w_up, w_conv, b_conv, tm, tf, seq):
    t, d = hn.shape
    ts = hs.shape[0]
    ff = w_gate.shape[1]
    tail = 8
    kern = functools.partial(_ffn_up_kernel, seq_tiles=seq // tm, row_split=4)
    col = lambda f, i: (0, f)
    return pl.pallas_call(
        kern,
        out_shape=(jax.ShapeDtypeStruct((t, ff), BF16),
                   jax.ShapeDtypeStruct((t // tm * tail, ff), F32),
                   jax.ShapeDtypeStruct((ts, ff), BF16),
                   jax.ShapeDtypeStruct((ts, ff), F32)),
        grid=(ff // tf, t // tm),
        in_specs=[pl.BlockSpec((tm, d), lambda f, i: (i, 0)),
                  pl.BlockSpec((d, tf), col),
                  pl.BlockSpec((d, tf), col),
                  pl.BlockSpec((CONV_W, tf), col),
                  pl.BlockSpec((1, tf), col),
                  pl.BlockSpec((ts, d), lambda f, i: (0, 0)),
                  pl.BlockSpec((ts, tf), col),
                  pl.BlockSpec((ts, tf), col)],
        out_specs=(pl.BlockSpec((tm, tf), lambda f, i: (i, f)),
                   pl.BlockSpec((tail, tf), lambda f, i: (i, f)),
                   pl.BlockSpec((ts, tf), col),
                   pl.BlockSpec((ts, tf), col)),
        scratch_shapes=[pltpu.VMEM((d, tf), BF16), pltpu.VMEM((d, tf), BF16),
                        pltpu.VMEM((8, tf), F32)],
        compiler_params=_cparams(("arbitrary", "arbitrary")),
        name="ffn_up",
    )(hn, w_gate, w_up, w_conv, b_conv, hs, hist[0], hist[1])


def _ffn_down_kernel(a_ref, w_ref, x_ref, g_ref, as_ref, xs_ref, y_ref, ys_ref, *, tn):
    i = pl.program_id(0)
    k = pl.program_id(1)
    last = pl.num_programs(1) - 1

    @pl.when(k == 0)
    def _():
        y_ref[...] = x_ref[...]

    a = a_ref[...]
    for c in range(y_ref.shape[1] // tn):
        cols = pl.ds(c * tn, tn)
        y_ref[:, cols] += _dot(a, w_ref[:, cols])

    @pl.when(k == last)
    def _():
        y_ref[...] = _rms(y_ref[...], g_ref[...])

    @pl.when(i == 0)
    def _():
        @pl.when(k == 0)
        def _():
            ys_ref[...] = xs_ref[...]

        ys_ref[...] += _dot(as_ref[...], w_ref[...])

        @pl.when(k == last)
        def _():
            ys_ref[...] = _rms(ys_ref[...], g_ref[...])


def _ffn_down(a, a_s, w_down, x, x_s, g_final, tm, tk, tn):
    t, d = x.shape
    ts = x_s.shape[0]
    ff = a.shape[1]
    return pl.pallas_call(
        functools.partial(_ffn_down_kernel, tn=tn),
        out_shape=(jax.ShapeDtypeStruct((t, d), F32), jax.ShapeDtypeStruct((ts, d), F32)),
        grid=(t // tm, ff // tk),
        in_specs=[pl.BlockSpec((tm, tk), lambda i, k: (i, k)),
                  pl.BlockSpec((tk, d), lambda i, k: (k, 0)),
                  pl.BlockSpec((tm, d), lambda i, k: (i, 0)),
                  pl.BlockSpec((1, d), lambda i, k: (0, 0)),
                  pl.BlockSpec((ts, tk), lambda i, k: (0, k)),
                  pl.BlockSpec((ts, d), lambda i, k: (0, 0))],
        out_specs=(pl.BlockSpec((tm, d), lambda i, k: (i, 0)),
                   pl.BlockSpec((ts, d), lambda i, k: (0, 0))),
        compiler_params=_cparams(("arbitrary", "arbitrary")),
        name="ffn_down",
    )(a, w_down, x, g_final, a_s, x_s)


def _rope_tables(pos):
    inv = ROPE_THETA ** (-jnp.arange(0, MLA_ROPE, 2, dtype=F32) / MLA_ROPE)
    ang = pos.astype(F32)[:, None] * inv[None, :]
    cos, sin = jnp.cos(ang), jnp.sin(ang)
    zero = jnp.zeros_like(cos)
    return (jnp.concatenate([cos, cos, zero, zero], axis=1),
            jnp.concatenate([-sin, sin, zero, zero], axis=1))


def _prep_w_in(w_in, q_lora, kv_rank, dk, dv):
    nh = ML_HEADS
    o_kr = q_lora + kv_rank
    o_qm = o_kr + MLA_ROPE
    o_gate = o_qm + 2 * nh * dk + 2 * nh * dv
    half = MLA_ROPE // 2
    w_t = w_in.T.astype(BF16)
    pad = jnp.zeros((LANES - 2 * nh, w_in.shape[0]), BF16)
    rows = [w_t[:o_kr], w_t[o_qm:o_gate],
            w_t[o_kr:o_qm], w_t[o_kr + half:o_qm], w_t[o_kr:o_kr + half],
            w_t[o_gate:], pad]
    return jnp.concatenate(rows, axis=0)


def _prep_w_uq(w_uq):
    hd = MLA_NOPE + MLA_ROPE
    half = MLA_ROPE // 2
    cols = []
    for h in range(MLA_HEADS):
        o = h * hd
        cols += [w_uq[:, o:o + hd], w_uq[:, o + MLA_NOPE + half:o + hd],
                 w_uq[:, o + MLA_NOPE:o + MLA_NOPE + half]]
    return jnp.concatenate(cols, axis=1).astype(BF16)


def _mixer(x, pos, lw, prep, mode, state):
    (g_mix, w_in, g_q, w_uq, g_kv, w_uk, w_uv, b_i, b_f, g_ml, w_out,
     g_ffn, w_gate, w_up, w_conv, b_conv, w_down) = lw
    w_in_p, w_uq_p, w_uk_f, w_uv_f, w_out_b, w_down_b = prep
    t, d = x.shape
    nh = ML_HEADS
    dk = (w_in.shape[1] - g_q.shape[0] - g_kv.shape[0] - MLA_ROPE - 2 * nh) // (6 * nh)
    dv = 2 * dk
    prompt = mode == "prompt"
    tm = 512 if prompt else t
    cos, sin = _rope_tables(pos)
    if not prompt:
        cos = jnp.broadcast_to(cos, (t, LANES))
        sin = jnp.broadcast_to(sin, (t, LANES))

    z = _proj_in(x, g_mix[None, :], w_in_p, tm, w_in_p.shape[0] // 2)
    q_cat, k_cat, v, ckv, kr = _mla_proj(
        z, g_q[None, :], g_kv[None, :], w_uq_p, w_uk_f.astype(BF16), w_uv_f.T.astype(BF16),
        cos, sin, tm)
    bias_r = jnp.concatenate([b_i, b_f])[None, :].astype(F32)

    if prompt:
        batch, seq = state["batch"], state["seq"]
        o_mla = _flash(q_cat, k_cat, v, batch, seq, min(seq, 1024), 2)
        gates_row = z[:, 33 * LANES:33 * LANES + 2 * nh].T
        h_ml, c_new, n_new, m_new = _mlstm_prompt(
            z, gates_row, bias_r, bias_r.T, g_ml[None, :], batch, seq, math.gcd(seq, 256), dk, dv)
        n_new = n_new[:, :, 0, :]
        m_new = m_new[:, :, 0, 0]
    else:
        r = g_kv.shape[0]
        q_lat = _q_absorb(q_cat, w_uk_f).reshape(t, MLA_HEADS, r)
        q_rope = q_cat.reshape(t, MLA_HEADS, QK_SLOT)[:, :, MLA_NOPE:MLA_NOPE + MLA_ROPE].astype(F32)
        o_lat = _mla_decode(state["page_table"], q_lat, q_rope, ckv[:, None, :], kr[:, None, :],
                            state["cache_ckv"], state["cache_kr_t"], state["layer"], 16)
        o_mla = _o_absorb(o_lat.reshape(t, MLA_HEADS * r), w_uv_f)
        h_ml, c_new, n_new, m_new = _mlstm_sample(
            z[:, None, :], bias_r, g_ml[None, :], state["C"], state["n"], state["m"][:, None, :],
            dk, dv, 4)
        h_ml = h_ml[:, 0, :]
        m_new = m_new[:, 0, :]

    x1, hn = _out_proj(o_mla, h_ml, w_out_b, x, g_ffn[None, :], min(t, 512), 512)
    return x1, hn, (ckv, kr, c_new, n_new, m_new)


def _conv_ffn(x1_p, hn_p, x1_s, hn_s, conv0_s, lw, prep, g_out, batch, seq):
    w_gate, w_up, w_conv, b_conv = lw[12:16]
    w_down_b = prep[5]
    tmf = min(seq, 1024)
    a_p, u_tail, a_s, u_s = _ffn_up(hn_p, hn_s, (conv0_s[:, 0, :], conv0_s[:, 1, :]),
                                    w_gate, w_up, w_conv, b_conv[None, :], tmf, 512, seq)
    tiles = seq // tmf
    u_tail = u_tail.reshape(batch, tiles, 8, -1)
    conv_p = u_tail[:, tiles - 1, 8 - (CONV_W - 1):, :]
    conv_s = jnp.stack([conv0_s[:, 1, :], u_s], axis=1)
    y_p, y_s = _ffn_down(a_p, a_s, w_down_b, x1_p, x1_s, g_out[None, :], tmf, 1408, 512)
    return y_p, y_s, conv_p, conv_s


def kernel(x_prompt, x_sample, cache_ckv, cache_kr, state_C, state_n, state_m, state_conv, page_table, g_mix, w_in, g_q, w_uq, g_kv, w_uk, w_uv, b_i, b_f, g_ml, w_out, g_ffn, w_gate, w_up, w_conv, b_conv, w_down, g_final):
    depth = w_in.shape[0]
    assert depth == 1, "the final RMSNorm is fused into the (single) layer's FFN kernel"
    bp, sp, d = x_prompt.shape
    bs, ss, _ = x_sample.shape
    assert ss == 1, "the sample group decodes one token per sequence"
    past_len = page_table.shape[1] * cache_ckv.shape[2]
    pos_p = jnp.arange(sp)
    pos_s = past_len + jnp.arange(ss)
    l = 0
    lw = (g_mix[l], w_in[l], g_q[l], w_uq[l], g_kv[l], w_uk[l], w_uv[l], b_i[l], b_f[l],
          g_ml[l], w_out[l], g_ffn[l], w_gate[l], w_up[l], w_conv[l], b_conv[l], w_down[l])
    q_lora, kv_rank = g_q.shape[1], g_kv.shape[1]
    nh = ML_HEADS
    dk = (w_in.shape[2] - q_lora - kv_rank - MLA_ROPE - 2 * nh) // (6 * nh)
    prep = (_prep_w_in(w_in[l], q_lora, kv_rank, dk, 2 * dk), _prep_w_uq(w_uq[l]),
            w_uk[l].reshape(kv_rank, -1), w_uv[l].reshape(kv_rank, -1), w_out[l].astype(BF16),
            w_down[l].astype(BF16))
    cache_kr_t = jnp.swapaxes(cache_kr, 2, 3)

    x1_p, hn_p, new_p = _mixer(x_prompt.reshape(bp * sp, d), pos_p, lw, prep, "prompt",
                               dict(batch=bp, seq=sp))
    x1_s, hn_s, new_s = _mixer(x_sample.reshape(bs * ss, d), pos_s, lw, prep, "sample",
                               dict(page_table=page_table, cache_ckv=cache_ckv,
                                    cache_kr_t=cache_kr_t, layer=l, C=state_C[l],
                                    n=state_n[l], m=state_m[l]))
    yp, ys, conv_p, conv_s = _conv_ffn(x1_p, hn_p, x1_s, hn_s, state_conv[l], lw, prep,
                                       g_final, bp, sp)
    ckv_p, kr_p, c_p, n_p, m_p = new_p
    ckv_s, kr_s, c_s, n_s, m_s = new_s
    return (yp.reshape(bp, sp, d), ys.reshape(bs, ss, d),
            ckv_p.reshape(1, bp, sp, -1), kr_p.reshape(1, bp, sp, -1),
            c_p[None], n_p[None], m_p[None], conv_p[None],
            ckv_s.reshape(1, bs, ss, -1), kr_s.reshape(1, bs, ss, -1),
            c_s[None], n_s[None], m_s[None], conv_s[None])
```

```python
import functools
import math

import jax
import jax.numpy as jnp
from jax import lax
from jax.experimental import pallas as pl
from jax.experimental.pallas import tpu as pltpu

F32 = jnp.float32
BF16 = jnp.bfloat16

MLA_HEADS = 8
MLA_NOPE = 128
MLA_ROPE = 64
MLA_V = 128
ML_HEADS = 4
CONV_W = 3
ROPE_THETA = 10000.0
EPS = 1e-6
SCALE = (MLA_NOPE + MLA_ROPE) ** -0.5
LOG2E = math.log2(math.e)

LANES = 128
QK_SLOT = 2 * LANES
VMEM_LIMIT = 56 * 1024 * 1024


def _cparams(sem, vmem=VMEM_LIMIT):
    return pltpu.CompilerParams(dimension_semantics=sem, vmem_limit_bytes=vmem)


def _rms(x, g):
    return x * lax.rsqrt(jnp.mean(x * x, axis=-1, keepdims=True) + EPS) * g


def _sigmoid(x):
    return 1.0 / (1.0 + jnp.exp(-x))


def _log_sigmoid(x):
    return jnp.minimum(x, 0.0) - jnp.log1p(jnp.exp(-jnp.abs(x)))


def _dot(a, b):
    return jnp.dot(a, b, preferred_element_type=F32)


def _dot_nt(a, b):
    return lax.dot_general(a, b, (((1,), (1,)), ((), ())), preferred_element_type=F32)


def _dot_tn(a, b):
    return lax.dot_general(a, b, (((0,), (0,)), ((), ())), preferred_element_type=F32)


def _proj_in_kernel(x_ref, g_ref, w_ref, z_ref, xn_ref):
    @pl.when(pl.program_id(1) == 0)
    def _():
        xn_ref[...] = _rms(x_ref[...], g_ref[...]).astype(BF16)

    z_ref[...] = _dot_nt(xn_ref[...], w_ref[...])


def _proj_in(x, g, w_t, tm, tn):
    t, d = x.shape
    n = w_t.shape[0]
    return pl.pallas_call(
        _proj_in_kernel,
        out_shape=jax.ShapeDtypeStruct((t, n), F32),
        grid=(t // tm, n // tn),
        in_specs=[pl.BlockSpec((tm, d), lambda i, j: (i, 0)),
                  pl.BlockSpec((1, d), lambda i, j: (0, 0)),
                  pl.BlockSpec((tn, d), lambda i, j: (j, 0))],
        out_specs=pl.BlockSpec((tm, tn), lambda i, j: (i, j)),
        scratch_shapes=[pltpu.VMEM((tm, d), BF16)],
        compiler_params=_cparams(("arbitrary", "arbitrary")),
        name="proj_in",
    )(x, g, w_t)


def _rope128(x, cos, sin):
    return x * cos + pltpu.roll(x, 64, axis=1) * sin


def _mla_proj_kernel(cq_ref, ckv_ref, tail_ref, gq_ref, gkv_ref, wq_ref, wk_ref, wv_ref,
                     cos_ref, sin_ref, q_ref, k_ref, v_ref, ckv_out_ref, kr_out_ref):
    cos = cos_ref[...]
    sin = sin_ref[...]
    cqn = _rms(cq_ref[...], gq_ref[...]).astype(BF16)
    q = _dot(cqn, wq_ref[...]) * (SCALE * LOG2E)
    ckv = _rms(ckv_ref[...], gkv_ref[...])
    ckv_out_ref[...] = ckv
    ckv_b = ckv.astype(BF16)
    kn = _dot(ckv_b, wk_ref[...])
    v_ref[...] = _dot_nt(wv_ref[...], ckv_b).astype(BF16)
    kr = _rope128(tail_ref[...], cos, sin)
    kr_out_ref[...] = kr[:, :MLA_ROPE]
    kr_b = kr.astype(BF16)
    for h in range(MLA_HEADS):
        q0 = h * QK_SLOT
        q_ref[:, q0:q0 + LANES] = q[:, q0:q0 + LANES].astype(BF16)
        q_ref[:, q0 + LANES:q0 + QK_SLOT] = _rope128(
            q[:, q0 + LANES:q0 + QK_SLOT], cos, sin).astype(BF16)
        k_ref[:, q0:q0 + LANES] = kn[:, h * MLA_NOPE:(h + 1) * MLA_NOPE].astype(BF16)
        k_ref[:, q0 + LANES:q0 + QK_SLOT] = kr_b


def _mla_proj(z, g_q, g_kv, wq, wk, wv_t, cos, sin, tm):
    wv = wv_t
    t = z.shape[0]
    rows = cos.shape[0] // tm
    hq = MLA_HEADS * QK_SLOT
    qlr, kvr = wq.shape[0], wk.shape[0]
    assert qlr == kvr, "c_q and c_kv are addressed as equal-width column blocks of z"
    tail_blk = (z.shape[1] - 2 * LANES) // LANES
    const = lambda i: (0, 0)
    return pl.pallas_call(
        _mla_proj_kernel,
        out_shape=(jax.ShapeDtypeStruct((t, hq), BF16),
                   jax.ShapeDtypeStruct((t, hq), BF16),
                   jax.ShapeDtypeStruct((MLA_HEADS * MLA_V, t), BF16),
                   jax.ShapeDtypeStruct((t, kvr), F32),
                   jax.ShapeDtypeStruct((t, MLA_ROPE), F32)),
        grid=(t // tm,),
        in_specs=[pl.BlockSpec((tm, qlr), lambda i: (i, 0)),
                  pl.BlockSpec((tm, kvr), lambda i: (i, 1)),
                  pl.BlockSpec((tm, LANES), lambda i: (i, tail_blk)),
                  pl.BlockSpec((1, qlr), const),
                  pl.BlockSpec((1, kvr), const),
                  pl.BlockSpec(wq.shape, const),
                  pl.BlockSpec(wk.shape, const),
                  pl.BlockSpec(wv.shape, const),
                  pl.BlockSpec((tm, LANES), lambda i: (i % rows, 0)),
                  pl.BlockSpec((tm, LANES), lambda i: (i % rows, 0))],
        out_specs=(pl.BlockSpec((tm, hq), lambda i: (i, 0)),
                   pl.BlockSpec((tm, hq), lambda i: (i, 0)),
                   pl.BlockSpec((MLA_HEADS * MLA_V, tm), lambda i: (0, i)),
                   pl.BlockSpec((tm, kvr), lambda i: (i, 0)),
                   pl.BlockSpec((tm, MLA_ROPE), lambda i: (i, 0))),
        compiler_params=_cparams(("arbitrary",)),
        name="mla_proj",
    )(z, z, z, g_q, g_kv, wq, wk, wv, cos, sin)


def _flash_kernel(q_ref, k_ref, vt_ref, o_ref, m_ref, l_ref, acc_ref, *, heads):
    qi = pl.program_id(2)
    tq = q_ref.shape[0]
    m_ref[...] = jnp.full(m_ref.shape, -jnp.inf, F32)
    l_ref[...] = jnp.zeros(l_ref.shape, F32)
    acc_ref[...] = jnp.zeros(acc_ref.shape, F32)

    def update(j, diagonal):
        keys = pl.ds(pl.multiple_of(j * tq, tq), tq)
        for h in range(heads):
            k = k_ref[keys, h * QK_SLOT:(h + 1) * QK_SLOT]
            vt = vt_ref[h * MLA_V:(h + 1) * MLA_V, keys]
            st = _dot_nt(k, q_ref[:, h * QK_SLOT:(h + 1) * QK_SLOT])
            if diagonal:
                key = lax.broadcasted_iota(jnp.int32, st.shape, 0)
                qry = lax.broadcasted_iota(jnp.int32, st.shape, 1)
                st = jnp.where(key <= qry, st, -jnp.inf)
            m_prev = m_ref[h]
            m_new = jnp.maximum(m_prev, jnp.max(st, axis=0, keepdims=True))
            alpha = jnp.exp2(m_prev - m_new)
            pt = jnp.exp2(st - m_new)
            l_ref[h] = alpha * l_ref[h] + jnp.sum(pt, axis=0, keepdims=True)
            acc_ref[h] = alpha * acc_ref[h] + _dot(vt, pt.astype(BF16))
            m_ref[h] = m_new

    def body(j, carry):
        update(j, False)
        return carry

    lax.fori_loop(0, qi, body, 0)
    update(qi, True)
    for h in range(heads):
        o_ref[:, h * MLA_V:(h + 1) * MLA_V] = (acc_ref[h] / l_ref[h]).T.astype(o_ref.dtype)


def _flash(q, k, v_t, batch, seq, tq, heads):
    nb = seq // tq
    t = q.shape[0]
    hg = MLA_HEADS // heads
    return pl.pallas_call(
        functools.partial(_flash_kernel, heads=heads),
        out_shape=jax.ShapeDtypeStruct((t, MLA_HEADS * MLA_V), BF16),
        grid=(batch, hg, nb),
        in_specs=[pl.BlockSpec((tq, heads * QK_SLOT), lambda b, h, i: (b * nb + i, h)),
                  pl.BlockSpec((seq, heads * QK_SLOT), lambda b, h, i: (b, h)),
                  pl.BlockSpec((heads * MLA_V, seq), lambda b, h, i: (h, b))],
        out_specs=pl.BlockSpec((tq, heads * MLA_V), lambda b, h, i: (b * nb + i, h)),
        scratch_shapes=[pltpu.VMEM((heads, 1, tq), F32), pltpu.VMEM((heads, 1, tq), F32),
                        pltpu.VMEM((heads, MLA_V, tq), F32)],
        compiler_params=_cparams(("arbitrary",) * 3),
        name="flash_attn",
    )(q, k, v_t)


def _mlstm_chunk_kernel(q_ref, k_ref, v_ref, o_ref, gc_ref, gr_ref, bias_r_ref, bias_c_ref,
                        gml_ref, h_ref, c_out_ref, n_out_ref, m_out_ref,
                        c_sc, n_sc, m_sc, *, dk, dv):
    c_idx = pl.program_id(0)
    L = q_ref.shape[1]
    nh = ML_HEADS

    @pl.when(c_idx == 0)
    def _():
        c_sc[...] = jnp.zeros(c_sc.shape, F32)
        n_sc[...] = jnp.zeros(n_sc.shape, F32)
        m_sc[...] = jnp.zeros(m_sc.shape, F32)

    row = lax.broadcasted_iota(jnp.int32, (L, L), 0)
    col = lax.broadcasted_iota(jnp.int32, (L, L), 1)
    lower = col <= row
    lower_f = lower.astype(F32)
    upper_f = (row <= col).astype(F32)
    hp = lax.Precision.HIGHEST
    for b, h in [(b, h) for b in range(q_ref.shape[0]) for h in range(nh)]:
        if h == 0:
            gates_c = gc_ref[b][:, :2 * nh] + bias_r_ref[...]
            gates_r = gr_ref[b] + bias_c_ref[...]
            b_col = jnp.dot(lower_f, _log_sigmoid(gates_c[:, nh:]), precision=hp,
                            preferred_element_type=F32)
            b_row = jnp.dot(_log_sigmoid(gates_r[nh:, :]), upper_f, precision=hp,
                            preferred_element_type=F32)
        bc = b_col[:, h:h + 1]
        br = b_row[h:h + 1, :]
        ic = gates_c[:, h:h + 1]
        ir = gates_r[h:h + 1, :]
        m_prev = m_sc[b, h][0:1, 0:1]
        log_d = jnp.where(lower, bc - br + ir, -jnp.inf)
        log_prev = bc + m_prev
        m_t = jnp.maximum(log_prev, jnp.max(log_d, axis=1, keepdims=True))
        q = q_ref[b, :, h * dk:(h + 1) * dk]
        k = k_ref[b, :, h * dk:(h + 1) * dk] * (dk ** -0.5)
        v = v_ref[b, :, h * dv:(h + 1) * dv]
        qb = q.astype(BF16)
        kb = k.astype(BF16)
        s = _dot_nt(qb, kb) * jnp.exp(log_d - m_t)
        w_prev = jnp.exp(log_prev - m_t)
        c_prev = c_sc[b, h]
        n_prev = n_sc[b, h][0:1, :]
        num = _dot(s.astype(BF16), v.astype(BF16)) + w_prev * _dot_nt(qb, c_prev.astype(BF16))
        den = jnp.sum(s, axis=1, keepdims=True) + w_prev * jnp.sum(q * n_prev, axis=1, keepdims=True)
        hh = num / jnp.maximum(jnp.abs(den), jnp.exp(-m_t))
        hh = _rms(hh, gml_ref[:, h * dv:(h + 1) * dv])
        h_ref[b, :, h * dv:(h + 1) * dv] = (hh * _sigmoid(o_ref[b, :, h * dv:(h + 1) * dv])).astype(h_ref.dtype)
        m_new = m_t[L - 1:L, :]
        b_last = bc[L - 1:L, :]
        w_s = jnp.exp(b_last - bc + ic - m_new)
        decay = jnp.exp(b_last + m_prev - m_new)
        c_sc[b, h] = decay * c_prev + _dot_tn((w_s * v).astype(BF16), kb)
        n_sc[b, h] = jnp.broadcast_to(decay * n_prev + jnp.sum(w_s * k, axis=0, keepdims=True),
                                      n_sc.shape[2:])
        m_sc[b, h] = jnp.broadcast_to(m_new, m_sc.shape[2:])

    @pl.when(c_idx == pl.num_programs(0) - 1)
    def _():
        c_out_ref[...] = c_sc[...]
        n_out_ref[...] = n_sc[...]
        m_out_ref[...] = m_sc[...]


def _mlstm_prompt(z, bias_r, bias_c, g_ml, batch, seq, chunk, dk, dv):
    nc = seq // chunk
    nh = ML_HEADS
    z3 = z.reshape(batch, seq, z.shape[1])
    gate_blk = z.shape[1] // LANES - 1
    gates_row = jnp.swapaxes(z3[:, :, gate_blk * LANES:gate_blk * LANES + 2 * nh], 1, 2)
    kern = functools.partial(_mlstm_chunk_kernel, dk=dk, dv=dv)
    tok = lambda w, blk: pl.BlockSpec((batch, chunk, w), lambda c: (0, c, blk))
    const = lambda c: (0, 0)
    st = lambda c: (0, 0, 0, 0)
    h, c_new, n_new, m_new = pl.pallas_call(
        kern,
        out_shape=(jax.ShapeDtypeStruct((batch, seq, nh * dv), BF16),
                   jax.ShapeDtypeStruct((batch, nh, dv, dk), F32),
                   jax.ShapeDtypeStruct((batch, nh, 8, dk), F32),
                   jax.ShapeDtypeStruct((batch, nh, 8, LANES), F32)),
        grid=(nc,),
        in_specs=[tok(nh * dk, 2), tok(nh * dk, 3), tok(nh * dv, 2), tok(nh * dv, 3),
                  tok(LANES, gate_blk),
                  pl.BlockSpec((batch, 2 * nh, chunk), lambda c: (0, 0, c)),
                  pl.BlockSpec((1, 2 * nh), const),
                  pl.BlockSpec((2 * nh, 1), const),
                  pl.BlockSpec((1, nh * dv), const)],
        out_specs=(pl.BlockSpec((batch, chunk, nh * dv), lambda c: (0, c, 0)),
                   pl.BlockSpec((batch, nh, dv, dk), st),
                   pl.BlockSpec((batch, nh, 8, dk), st),
                   pl.BlockSpec((batch, nh, 8, LANES), st)),
        scratch_shapes=[pltpu.VMEM((batch, nh, dv, dk), F32),
                        pltpu.VMEM((batch, nh, 8, dk), F32),
                        pltpu.VMEM((batch, nh, 8, LANES), F32)],
        compiler_params=_cparams(("arbitrary",)),
        name="mlstm_chunk",
    )(z3, z3, z3, z3, z3, gates_row, bias_r, bias_c, g_ml)
    return h.reshape(batch * seq, nh * dv), c_new, n_new, m_new


def _mlstm_step_kernel(q_ref, k_ref, v_ref, o_ref, g_ref, bias_ref, gml_ref, c0_ref, n0_ref,
                       m0_ref, h_ref, c_ref, n_ref, m_ref, *, dk, dv):
    nh = ML_HEADS
    bd_lo = lax.broadcasted_iota(jnp.int32, (8, nh * dk), 0) * dk
    bd_col = lax.broadcasted_iota(jnp.int32, (8, nh * dk), 1)
    bd_mask = (bd_col >= bd_lo) & (bd_col < bd_lo + dk)
    v_row = lax.broadcasted_iota(jnp.int32, (8, dv), 0)
    for t in range(q_ref.shape[0]):
        gates = g_ref[t][:, :2 * nh] + bias_ref[...]
        m_all = m0_ref[t]
        n_all = n0_ref[t]
        q_all = q_ref[t]
        k_all = k_ref[t] * (dk ** -0.5)
        q_bd = jnp.where(bd_mask, jnp.broadcast_to(q_all, bd_mask.shape), 0.0).astype(BF16)
        k_bd = jnp.where(bd_mask, jnp.broadcast_to(k_all, bd_mask.shape), 0.0).astype(BF16)
        c_cat = jnp.concatenate([c0_ref[t, h] for h in range(nh)], axis=1)
        cq_all = _dot_nt(q_bd, c_cat.astype(BF16))
        m_cols = []
        wv_rows = jnp.zeros((8, dv), F32)
        decays = []
        for h in range(nh):
            i_g = gates[:, h:h + 1]
            b = _log_sigmoid(gates[:, nh + h:nh + h + 1])
            m_prev = m_all[:, h:h + 1]
            q = q_all[:, h * dk:(h + 1) * dk]
            k = k_all[:, h * dk:(h + 1) * dk]
            v = v_ref[t][:, h * dv:(h + 1) * dv]
            log_prev = b + m_prev
            m_t = jnp.maximum(log_prev, i_g)
            qk = jnp.sum(q.astype(BF16).astype(F32) * k.astype(BF16).astype(F32),
                         axis=1, keepdims=True)
            s = qk * jnp.exp(i_g - m_t)
            w_prev = jnp.exp(log_prev - m_t)
            n_prev = n_all[h:h + 1, :]
            num = s * v.astype(BF16).astype(F32) + w_prev * cq_all[h:h + 1, :]
            den = s + w_prev * jnp.sum(q * n_prev, axis=1, keepdims=True)
            hh = num / jnp.maximum(jnp.abs(den), jnp.exp(-m_t))
            hh = _rms(hh, gml_ref[:, h * dv:(h + 1) * dv])
            gate = _sigmoid(o_ref[t][:, h * dv:(h + 1) * dv])
            h_ref[t, :, h * dv:(h + 1) * dv] = (hh * gate).astype(h_ref.dtype)
            w_s = jnp.exp(i_g - m_t)
            decay = jnp.exp(log_prev - m_t)
            wv_rows = jnp.where(v_row == h, jnp.broadcast_to(w_s * v, (8, dv)), wv_rows)
            n_ref[t, h:h + 1, :] = decay * n_prev + w_s * k
            decays.append(decay)
            m_cols.append(m_t)
        outer = _dot_tn(wv_rows.astype(BF16), k_bd)
        for h in range(nh):
            c_ref[t, h] = decays[h] * c0_ref[t, h] + outer[:, h * dk:(h + 1) * dk]
        m_ref[t] = jnp.concatenate(m_cols, axis=1)


def _mlstm_sample(z3, bias_r, g_ml, c0, n0, m0, dk, dv, tb):
    nb = z3.shape[0]
    nh = ML_HEADS
    kern = functools.partial(_mlstm_step_kernel, dk=dk, dv=dv)
    tok = lambda w, blk: pl.BlockSpec((tb, 1, w), lambda b: (b, 0, blk))
    const = lambda b: (0, 0)
    return pl.pallas_call(
        kern,
        out_shape=(jax.ShapeDtypeStruct((nb, 1, nh * dv), BF16),
                   jax.ShapeDtypeStruct((nb, nh, dv, dk), F32),
                   jax.ShapeDtypeStruct((nb, nh, dk), F32),
                   jax.ShapeDtypeStruct((nb, 1, nh), F32)),
        grid=(nb // tb,),
        in_specs=[tok(nh * dk, 2), tok(nh * dk, 3), tok(nh * dv, 2), tok(nh * dv, 3),
                  tok(LANES, 33),
                  pl.BlockSpec((1, 2 * nh), const),
                  pl.BlockSpec((1, nh * dv), const),
                  pl.BlockSpec((tb, nh, dv, dk), lambda b: (b, 0, 0, 0)),
                  pl.BlockSpec((tb, nh, dk), lambda b: (b, 0, 0)),
                  pl.BlockSpec((tb, 1, nh), lambda b: (b, 0, 0))],
        out_specs=(pl.BlockSpec((tb, 1, nh * dv), lambda b: (b, 0, 0)),
                   pl.BlockSpec((tb, nh, dv, dk), lambda b: (b, 0, 0, 0)),
                   pl.BlockSpec((tb, nh, dk), lambda b: (b, 0, 0)),
                   pl.BlockSpec((tb, 1, nh), lambda b: (b, 0, 0))),
        compiler_params=_cparams(("arbitrary",)),
        name="mlstm_step",
    )(z3, z3, z3, z3, z3, bias_r, g_ml, c0, n0, m0)


def _q_absorb_kernel(q_ref, wk_ref, o_ref):
    o_ref[...] = _dot_nt(q_ref[:, :MLA_NOPE], wk_ref[...].astype(BF16))


def _q_absorb(q_cat, w_uk_flat):
    t = q_cat.shape[0]
    r = w_uk_flat.shape[0]
    return pl.pallas_call(
        _q_absorb_kernel,
        out_shape=jax.ShapeDtypeStruct((t, MLA_HEADS * r), F32),
        grid=(MLA_HEADS,),
        in_specs=[pl.BlockSpec((t, QK_SLOT), lambda h: (0, h)),
                  pl.BlockSpec((r, MLA_NOPE), lambda h: (0, h))],
        out_specs=pl.BlockSpec((t, r), lambda h: (0, h)),
        compiler_params=_cparams(("arbitrary",)),
        name="q_absorb",
    )(q_cat, w_uk_flat)


def _o_absorb_kernel(o_ref, wv_ref, out_ref):
    out_ref[...] = _dot(o_ref[...].astype(BF16), wv_ref[...].astype(BF16)).astype(out_ref.dtype)


def _o_absorb(o_lat2d, w_uv_flat):
    t = o_lat2d.shape[0]
    r = w_uv_flat.shape[0]
    return pl.pallas_call(
        _o_absorb_kernel,
        out_shape=jax.ShapeDtypeStruct((t, MLA_HEADS * MLA_V), BF16),
        grid=(MLA_HEADS,),
        in_specs=[pl.BlockSpec((t, r), lambda h: (0, h)),
                  pl.BlockSpec((r, MLA_V), lambda h: (0, h))],
        out_specs=pl.BlockSpec((t, MLA_V), lambda h: (0, h)),
        compiler_params=_cparams(("arbitrary",)),
        name="o_absorb",
    )(o_lat2d, w_uv_flat)


def _mla_decode_kernel(pt_ref, ql_ref, qr_ref, cn_ref, kn_ref, cc_hbm, ckt_hbm, o_ref,
                       cbuf, kbuf, sem, *, layer, n_seq, n_chunks, pages_per_chunk, page_size,
                       group, n_slots):
    total = (n_seq // group) * n_chunks
    pages_per_seq = n_chunks * pages_per_chunk
    kc = pages_per_chunk * page_size
    ahead = n_slots - 1

    def chunk_copies(step, slot):
        g = step // n_chunks
        c = step - g * n_chunks
        cps = []
        for a in range(group):
            base = (g * group + a) * pages_per_seq + c * pages_per_chunk
            for j in range(pages_per_chunk):
                page = pt_ref[base + j]
                dst = pl.ds(a * kc + j * page_size, page_size)
                cps.append(pltpu.make_async_copy(
                    cc_hbm.at[layer, page], cbuf.at[slot, dst], sem.at[0, slot]))
                cps.append(pltpu.make_async_copy(
                    ckt_hbm.at[layer, page], kbuf.at[slot, :, dst], sem.at[1, slot]))
        return cps

    def issue(step):
        if isinstance(step, int):
            src, slot = min(step, total - 1), step % n_slots
        else:
            src, slot = jnp.minimum(step, total - 1), lax.rem(step, n_slots)
        for cp in chunk_copies(src, slot):
            cp.start()

    for step in range(ahead):
        issue(step)

    def group_body(g, _):
        ql = [ql_ref[g * group + a] for a in range(group)]
        qr = [qr_ref[g * group + a] for a in range(group)]
        ql_b = [x.astype(BF16) for x in ql]
        qr_b = [x.astype(BF16) for x in qr]

        def chunk_body(c, carry):
            step = g * n_chunks + c
            slot = lax.rem(step, n_slots)
            issue(step + ahead)
            for cp in chunk_copies(step, slot):
                cp.wait()
            out = []
            for a in range(group):
                m, l, acc = carry[a]
                keys = pl.ds(a * kc, kc)
                cb = cbuf[slot, keys, :].astype(BF16)
                kb = kbuf[slot, :, keys].astype(BF16)
                s = _dot_nt(ql_b[a], cb) + _dot(qr_b[a], kb)
                m_new = jnp.maximum(m, jnp.max(s, axis=1, keepdims=True))
                alpha = jnp.exp2(m - m_new)
                p = jnp.exp2(s - m_new)
                l = alpha * l + jnp.sum(p, axis=1, keepdims=True)
                acc = alpha * acc + _dot(p.astype(BF16), cb)
                out.append((m_new, l, acc))
            return tuple(out)

        nh = ql[0].shape[0]
        init = tuple((jnp.full((nh, 1), -jnp.inf, F32), jnp.zeros((nh, 1), F32),
                      jnp.zeros(ql[0].shape, F32)) for _ in range(group))
        fin = lax.fori_loop(0, n_chunks, chunk_body, init)
        for a in range(group):
            m, l, acc = fin[a]
            b = g * group + a
            cn = cn_ref[b]
            kn = kn_ref[b]
            s_new = (jnp.sum(ql[a] * cn, axis=1, keepdims=True)
                     + jnp.sum(qr[a] * kn, axis=1, keepdims=True))
            m_f = jnp.maximum(m, s_new)
            alpha = jnp.exp2(m - m_f)
            p_new = jnp.exp2(s_new - m_f)
            o_ref[b] = (alpha * acc + p_new * cn) / (alpha * l + p_new)
        return 0

    lax.fori_loop(0, n_seq // group, group_body, 0)
    for step in range(total, total + ahead):
        for cp in chunk_copies(total - 1, step % n_slots):
            cp.wait()


def _mla_decode(page_table, q_lat, q_rope, ckv_new, kr_new, cache_ckv, cache_kr_t, layer,
                pages_per_chunk, group=2, n_slots=3):
    n_seq, n_pages = page_table.shape
    page_size, r = cache_ckv.shape[2], cache_ckv.shape[3]
    rope = cache_kr_t.shape[2]
    assert n_seq % group == 0 and n_pages % pages_per_chunk == 0
    n_chunks = n_pages // pages_per_chunk
    kc = pages_per_chunk * page_size
    kern = functools.partial(_mla_decode_kernel, layer=layer, n_seq=n_seq, n_chunks=n_chunks,
                             pages_per_chunk=pages_per_chunk, page_size=page_size,
                             group=group, n_slots=n_slots)
    vmem = pl.BlockSpec(memory_space=pltpu.VMEM)
    return pl.pallas_call(
        kern,
        out_shape=jax.ShapeDtypeStruct(q_lat.shape, F32),
        in_specs=[pl.BlockSpec(memory_space=pltpu.SMEM), vmem, vmem, vmem, vmem,
                  pl.BlockSpec(memory_space=pl.ANY), pl.BlockSpec(memory_space=pl.ANY)],
        out_specs=vmem,
        scratch_shapes=[pltpu.VMEM((n_slots, group * kc, r), F32),
                        pltpu.VMEM((n_slots, rope, group * kc), F32),
                        pltpu.SemaphoreType.DMA((2, n_slots))],
        compiler_params=pltpu.CompilerParams(vmem_limit_bytes=VMEM_LIMIT),
        name="mla_decode",
    )(page_table.reshape(-1), q_lat, q_rope, ckv_new, kr_new, cache_ckv, cache_kr_t)


def _out_proj_kernel(a_ref, h_ref, w1_ref, w2_ref, x_ref, g_ref, o_ref, hn_ref, *, tn):
    a = a_ref[...]
    h = h_ref[...]
    for c in range(o_ref.shape[1] // tn):
        cols = pl.ds(c * tn, tn)
        o_ref[:, cols] = x_ref[:, cols] + _dot(a, w1_ref[:, cols]) + _dot(h, w2_ref[:, cols])
    hn_ref[...] = _rms(o_ref[...], g_ref[...]).astype(hn_ref.dtype)


def _out_proj(a, h, w_out, x, g_next, tm, tn):
    t, d = x.shape
    ka = a.shape[1]
    kh = h.shape[1]
    assert ka == kh
    return pl.pallas_call(
        functools.partial(_out_proj_kernel, tn=tn),
        out_shape=(jax.ShapeDtypeStruct((t, d), F32), jax.ShapeDtypeStruct((t, d), BF16)),
        grid=(t // tm,),
        in_specs=[pl.BlockSpec((tm, ka), lambda i: (i, 0)),
                  pl.BlockSpec((tm, kh), lambda i: (i, 0)),
                  pl.BlockSpec((ka, d), lambda i: (0, 0)),
                  pl.BlockSpec((kh, d), lambda i: (1, 0)),
                  pl.BlockSpec((tm, d), lambda i: (i, 0)),
                  pl.BlockSpec((1, d), lambda i: (0, 0))],
        out_specs=(pl.BlockSpec((tm, d), lambda i: (i, 0)),
                   pl.BlockSpec((tm, d), lambda i: (i, 0))),
        compiler_params=_cparams(("arbitrary",)),
        name="out_proj",
    )(a, h, w_out, w_out, x, g_next)


def _ffn_up_kernel(hn_ref, wg_ref, wu_ref, wc_ref, bc_ref, hs_ref, h0_ref, h1_ref,
                   a_ref, u_ref, as_ref, us_ref, wgb_ref, wub_ref, prev_ref,
                   *, seq_tiles, row_split):
    i = pl.program_id(1)
    wc = wc_ref[...]
    bc = bc_ref[...]

    @pl.when(i == 0)
    def _():
        wgb_ref[...] = wg_ref[...].astype(BF16)
        wub_ref[...] = wu_ref[...].astype(BF16)
        prev_ref[...] = jnp.zeros(prev_ref.shape, F32)
        u = _dot(hs_ref[...], wgb_ref[...])
        up = _dot(hs_ref[...], wub_ref[...])
        us_ref[...] = u
        conv = h0_ref[...] * wc[0:1, :] + h1_ref[...] * wc[1:2, :] + u * wc[2:3, :] + bc
        as_ref[...] = (conv * _sigmoid(conv) * up).astype(as_ref.dtype)

    wg = wgb_ref[...]
    wu = wub_ref[...]
    tm = hn_ref.shape[0]
    tail = u_ref.shape[0]
    prev = jnp.where(lax.rem(i, seq_tiles) != 0, prev_ref[...], 0.0)
    sub = tm // row_split
    for r in range(row_split):
        rows = pl.ds(r * sub, sub)
        hn = hn_ref[rows, :]
        u = _dot(hn, wg)
        up = _dot(hn, wu)
        row = lax.broadcasted_iota(jnp.int32, u.shape, 0)
        p6 = jnp.broadcast_to(prev[6:7, :], u.shape)
        p7 = jnp.broadcast_to(prev[7:8, :], u.shape)
        u_m1 = jnp.where(row == 0, p7, pltpu.roll(u, 1, axis=0))
        u_m2 = jnp.where(row == 0, p6, jnp.where(row == 1, p7, pltpu.roll(u, 2, axis=0)))
        conv = u_m2 * wc[0:1, :] + u_m1 * wc[1:2, :] + u * wc[2:3, :] + bc
        a_ref[rows, :] = (conv * _sigmoid(conv) * up).astype(a_ref.dtype)
        prev = u[sub - 8:, :]
    prev_ref[...] = prev
    u_ref[...] = prev[8 - tail:, :]


def _ffn_up(hn, hs, hist, w_gate, w_up, w_conv, b_conv, tm, tf, seq):
    t, d = hn.shape
    ts = hs.shape[0]
    ff = w_gate.shape[1]
    tail = 8
    kern = functools.partial(_ffn_up_kernel, seq_tiles=seq // tm, row_split=4)
    col = lambda f, i: (0, f)
    return pl.pallas_call(
        kern,
        out_shape=(jax.ShapeDtypeStruct((t, ff), BF16),
                   jax.ShapeDtypeStruct((t // tm * tail, ff), F32),
                   jax.ShapeDtypeStruct((ts, ff), BF16),
                   jax.ShapeDtypeStruct((ts, ff), F32)),
        grid=(ff // tf, t // tm),
        in_specs=[pl.BlockSpec((tm, d), lambda f, i: (i, 0)),
                  pl.BlockSpec((d, tf), col),
                  pl.BlockSpec((d, tf), col),
                  pl.BlockSpec((CONV_W, tf), col),
                  pl.BlockSpec((1, tf), col),
                  pl.BlockSpec((ts, d), lambda f, i: (0, 0)),
                  pl.BlockSpec((ts, tf), col),
                  pl.BlockSpec((ts, tf), col)],
        out_specs=(pl.BlockSpec((tm, tf), lambda f, i: (i, f)),
                   pl.BlockSpec((tail, tf), lambda f, i: (i, f)),
                   pl.BlockSpec((ts, tf), col),
                   pl.BlockSpec((ts, tf), col)),
        scratch_shapes=[pltpu.VMEM((d, tf), BF16), pltpu.VMEM((d, tf), BF16),
                        pltpu.VMEM((8, tf), F32)],
        compiler_params=_cparams(("arbitrary", "arbitrary")),
        name="ffn_up",
    )(hn, w_gate, w_up, w_conv, b_conv, hs, hist[0], hist[1])


def _ffn_down_kernel(a_ref, w_ref, x_ref, g_ref, as_ref, xs_ref, y_ref, ys_ref, *, tn):
    i = pl.program_id(0)
    k = pl.program_id(1)
    last = pl.num_programs(1) - 1

    @pl.when(k == 0)
    def _():
        y_ref[...] = x_ref[...]

    a = a_ref[...]
    for c in range(y_ref.shape[1] // tn):
        cols = pl.ds(c * tn, tn)
        y_ref[:, cols] += _dot(a, w_ref[:, cols])

    @pl.when(k == last)
    def _():
        y_ref[...] = _rms(y_ref[...], g_ref[...])

    @pl.when(i == 0)
    def _():
        @pl.when(k == 0)
        def _():
            ys_ref[...] = xs_ref[...]

        ys_ref[...] += _dot(as_ref[...], w_ref[...])

        @pl.when(k == last)
        def _():
            ys_ref[...] = _rms(ys_ref[...], g_ref[...])


def _ffn_down(a, a_s, w_down, x, x_s, g_final, tm, tk, tn):
    t, d = x.shape
    ts = x_s.shape[0]
    ff = a.shape[1]
    return pl.pallas_call(
        functools.partial(_ffn_down_kernel, tn=tn),
        out_shape=(jax.ShapeDtypeStruct((t, d), F32), jax.ShapeDtypeStruct((ts, d), F32)),
        grid=(t // tm, ff // tk),
        in_specs=[pl.BlockSpec((tm, tk), lambda i, k: (i, k)),
                  pl.BlockSpec((tk, d), lambda i, k: (k, 0)),
                  pl.BlockSpec((tm, d), lambda i, k: (i, 0)),
                  pl.BlockSpec((1, d), lambda i, k: (0, 0)),
                  pl.BlockSpec((ts, tk), lambda i, k: (0, k)),
                  pl.BlockSpec((ts, d), lambda i, k: (0, 0))],
        out_specs=(pl.BlockSpec((tm, d), lambda i, k: (i, 0)),
                   pl.BlockSpec((ts, d), lambda i, k: (0, 0))),
        compiler_params=_cparams(("arbitrary", "arbitrary")),
        name="ffn_down",
    )(a, w_down, x, g_final, a_s, x_s)


def _rope_tables(pos):
    inv = ROPE_THETA ** (-jnp.arange(0, MLA_ROPE, 2, dtype=F32) / MLA_ROPE)
    ang = pos.astype(F32)[:, None] * inv[None, :]
    cos, sin = jnp.cos(ang), jnp.sin(ang)
    zero = jnp.zeros_like(cos)
    return (jnp.concatenate([cos, cos, zero, zero], axis=1),
            jnp.concatenate([-sin, sin, zero, zero], axis=1))


def _prep_w_in(w_in, q_lora, kv_rank, dk, dv):
    nh = ML_HEADS
    o_kr = q_lora + kv_rank
    o_qm = o_kr + MLA_ROPE
    o_gate = o_qm + 2 * nh * dk + 2 * nh * dv
    half = MLA_ROPE // 2
    w_t = w_in.T.astype(BF16)
    pad = jnp.zeros((LANES - 2 * nh, w_in.shape[0]), BF16)
    rows = [w_t[:o_kr], w_t[o_qm:o_gate],
            w_t[o_kr:o_qm], w_t[o_kr + half:o_qm], w_t[o_kr:o_kr + half],
            w_t[o_gate:], pad]
    return jnp.concatenate(rows, axis=0)


def _prep_w_uq(w_uq):
    hd = MLA_NOPE + MLA_ROPE
    half = MLA_ROPE // 2
    cols = []
    for h in range(MLA_HEADS):
        o = h * hd
        cols += [w_uq[:, o:o + hd], w_uq[:, o + MLA_NOPE + half:o + hd],
                 w_uq[:, o + MLA_NOPE:o + MLA_NOPE + half]]
    return jnp.concatenate(cols, axis=1).astype(BF16)


def _mixer(x, pos, lw, prep, mode, state):
    (g_mix, w_in, g_q, w_uq, g_kv, w_uk, w_uv, b_i, b_f, g_ml, w_out,
     g_ffn, w_gate, w_up, w_conv, b_conv, w_down) = lw
    w_in_p, w_uq_p, w_uk_f, w_uv_f, w_out_b, w_down_b = prep
    t, d = x.shape
    nh = ML_HEADS
    dk = (w_in.shape[1] - g_q.shape[0] - g_kv.shape[0] - MLA_ROPE - 2 * nh) // (6 * nh)
    dv = 2 * dk
    prompt = mode == "prompt"
    tm = 512 if prompt else t
    cos, sin = _rope_tables(pos)
    if not prompt:
        cos = jnp.broadcast_to(cos, (t, LANES))
        sin = jnp.broadcast_to(sin, (t, LANES))

    z = _proj_in(x, g_mix[None, :], w_in_p, tm, w_in_p.shape[0] // 2)
    q_cat, k_cat, v, ckv, kr = _mla_proj(
        z, g_q[None, :], g_kv[None, :], w_uq_p, w_uk_f.astype(BF16), w_uv_f.T.astype(BF16),
        cos, sin, tm)
    bias_r = jnp.concatenate([b_i, b_f])[None, :].astype(F32)

    if prompt:
        batch, seq = state["batch"], state["seq"]
        o_mla = _flash(q_cat, k_cat, v, batch, seq, min(seq, 1024), 2)
        h_ml, c_new, n_new, m_new = _mlstm_prompt(
            z, bias_r, bias_r.T, g_ml[None, :], batch, seq, math.gcd(seq, 256), dk, dv)
        n_new = n_new[:, :, 0, :]
        m_new = m_new[:, :, 0, 0]
    else:
        r = g_kv.shape[0]
        q_lat = _q_absorb(q_cat, w_uk_f).reshape(t, MLA_HEADS, r)
        q_rope = q_cat.reshape(t, MLA_HEADS, QK_SLOT)[:, :, MLA_NOPE:MLA_NOPE + MLA_ROPE].astype(F32)
        o_lat = _mla_decode(state["page_table"], q_lat, q_rope, ckv[:, None, :], kr[:, None, :],
                            state["cache_ckv"], state["cache_kr_t"], state["layer"], 16)
        o_mla = _o_absorb(o_lat.reshape(t, MLA_HEADS * r), w_uv_f)
        h_ml, c_new, n_new, m_new = _mlstm_sample(
            z[:, None, :], bias_r, g_ml[None, :], state["C"], state["n"], state["m"][:, None, :],
            dk, dv, 4)
        h_ml = h_ml[:, 0, :]
        m_new = m_new[:, 0, :]

    x1, hn = _out_proj(o_mla, h_ml, w_out_b, x, g_ffn[None, :], min(t, 512), 512)
    return x1, hn, (ckv, kr, c_new, n_new, m_new)


def _conv_ffn(x1_p, hn_p, x1_s, hn_s, conv0_s, lw, prep, g_out, batch, seq):
    w_gate, w_up, w_conv, b_conv = lw[12:16]
    w_down_b = prep[5]
    tmf = min(seq, 1024)
    a_p, u_tail, a_s, u_s = _ffn_up(hn_p, hn_s, (conv0_s[:, 0, :], conv0_s[:, 1, :]),
                                    w_gate, w_up, w_conv, b_conv[None, :], tmf, 512, seq)
    tiles = seq // tmf
    u_tail = u_tail.reshape(batch, tiles, 8, -1)
    conv_p = u_tail[:, tiles - 1, 8 - (CONV_W - 1):, :]
    conv_s = jnp.stack([conv0_s[:, 1, :], u_s], axis=1)
    y_p, y_s = _ffn_down(a_p, a_s, w_down_b, x1_p, x1_s, g_out[None, :], tmf, 1408, 512)
    return y_p, y_s, conv_p, conv_s


def kernel(x_prompt, x_sample, cache_ckv, cache_kr, state_C, state_n, state_m, state_conv, page_table, g_mix, w_in, g_q, w_uq, g_kv, w_uk, w_uv, b_i, b_f, g_ml, w_out, g_ffn, w_gate, w_up, w_conv, b_conv, w_down, g_final):
    depth = w_in.shape[0]
    assert depth == 1, "the final RMSNorm is fused into the (single) layer's FFN kernel"
    bp, sp, d = x_prompt.shape
    bs, ss, _ = x_sample.shape
    assert ss == 1, "the sample group decodes one token per sequence"
    past_len = page_table.shape[1] * cache_ckv.shape[2]
    pos_p = jnp.arange(sp)
    pos_s = past_len + jnp.arange(ss)
    l = 0
    lw = (g_mix[l], w_in[l], g_q[l], w_uq[l], g_kv[l], w_uk[l], w_uv[l], b_i[l], b_f[l],
          g_ml[l], w_out[l], g_ffn[l], w_gate[l], w_up[l], w_conv[l], b_conv[l], w_down[l])
    q_lora, kv_rank = g_q.shape[1], g_kv.shape[1]
    nh = ML_HEADS
    dk = (w_in.shape[2] - q_lora - kv_rank - MLA_ROPE - 2 * nh) // (6 * nh)
    prep = (_prep_w_in(w_in[l], q_lora, kv_rank, dk, 2 * dk), _prep_w_uq(w_uq[l]),
            w_uk[l].reshape(kv_rank, -1), w_uv[l].reshape(kv_rank, -1), w_out[l].astype(BF16),
            w_down[l].astype(BF16))
    cache_kr_t = jnp.swapaxes(cache_kr, 2, 3)

    x1_p, hn_p, new_p = _mixer(x_prompt.reshape(bp * sp, d), pos_p, lw, prep, "prompt",
                               dict(batch=bp, seq=sp))
    x1_s, hn_s, new_s = _mixer(x_sample.reshape(bs * ss, d), pos_s, lw, prep, "sample",
                               dict(page_table=page_table, cache_ckv=cache_ckv,
                                    cache_kr_t=cache_kr_t, layer=l, C=state_C[l],
                                    n=state_n[l], m=state_m[l]))
    yp, ys, conv_p, conv_s = _conv_ffn(x1_p, hn_p, x1_s, hn_s, state_conv[l], lw, prep,
                                       g_final, bp, sp)
    ckv_p, kr_p, c_p, n_p, m_p = new_p
    ckv_s, kr_s, c_s, n_s, m_s = new_s
    return (yp.reshape(bp, sp, d), ys.reshape(bs, ss, d),
            ckv_p.reshape(1, bp, sp, -1), kr_p.reshape(1, bp, sp, -1),
            c_p[None], n_p[None], m_p[None], conv_p[None],
            ckv_s.reshape(1, bs, ss, -1), kr_s.reshape(1, bs, ss, -1),
            c_s[None], n_s[None], m_s[None], conv_s[None])
```
